```python
import math
import jax, jax.numpy as jnp
from jax import lax
import numpy as np

D_MODEL = 1024
BATCH = 8
SEQ = 4096
DEPTH = 4

GRID_W = 64
CTX_LEN = 256
D_MIX = 2 * D_MODEL

SSD_HEADS = 16
SSD_HEAD_DIM = 64
SSD_WIDTH = SSD_HEADS * SSD_HEAD_DIM
SSD_GROUPS = 4
SSD_HPG = SSD_HEADS // SSD_GROUPS
SSD_STATE = 128
SSD_GN = SSD_GROUPS * SSD_STATE
SSD_XBC = SSD_WIDTH + 2 * SSD_GN
SSD_DT = 2 * SSD_HEADS
SSD_CHUNK = 128
SHORT_CONV = 3

DIFF_HEADS = 4
DIFF_DIM = 64
DIFF_WIDTH = DIFF_HEADS * 2 * DIFF_DIM
DIFF_SCALE = DIFF_DIM ** -0.5
Q_BLOCK = 128
ROPE_BASE = 10000.0
ROPE_FREQS = DIFF_DIM // 4

HY_WIDTH = D_MIX - SSD_WIDTH - DIFF_WIDTH
HY_ORDER = 2
HY_BANDS = 16
HY_EMB = 1 + 2 * HY_BANDS
HY_HIDDEN = 64
HY_DECAY_FAST = 0.3
HY_DECAY_SLOW = 1.5
HY_DECAY_TARGET = 1e-2
HY_MIN_DECAY = math.log(HY_DECAY_TARGET) / HY_DECAY_SLOW
HY_MAX_DECAY = math.log(HY_DECAY_TARGET) / HY_DECAY_FAST

IN_SIZES = (SSD_WIDTH, SSD_XBC, SSD_DT, DIFF_WIDTH, DIFF_WIDTH, DIFF_WIDTH, DIFF_WIDTH, 3 * HY_WIDTH, HY_WIDTH)
IN_SPLITS = tuple(int(s) for s in np.cumsum(IN_SIZES)[:-1])
N_IN = sum(IN_SIZES)

ALPHA = (2 * DEPTH) ** 0.25
BETA = (8 * DEPTH) ** -0.25
LN_EPS = 1e-5
RMS_EPS = 1e-6
F32 = jnp.float32

kernel_name = "hybrid_ssd_diffattn_hyena_dit"


def _layer_norm(x):
    xf = x.astype(F32)
    xc = xf - jnp.mean(xf, -1, keepdims=True)
    var = jnp.mean(xc * xc, -1, keepdims=True)
    return (xc * lax.rsqrt(var + LN_EPS)).astype(x.dtype)


def _rms_norm(x):
    xf = x.astype(F32)
    return (xf * lax.rsqrt(jnp.mean(xf * xf, -1, keepdims=True) + RMS_EPS)).astype(x.dtype)


def _centred_dwconv(u, w, b):
    pad = w.shape[0] // 2
    y = lax.conv_general_dilated(u, w[:, None, :].astype(u.dtype), (1,), [(pad, pad)],
                                 dimension_numbers=('NWC', 'WIO', 'NWC'),
                                 feature_group_count=u.shape[-1])
    return y + b


def _axial_rope_tables(L, dtype):
    rows = L // GRID_W
    row = jnp.repeat(jnp.arange(rows), GRID_W).astype(F32)
    col = jnp.tile(jnp.arange(GRID_W), rows).astype(F32)
    inv = ROPE_BASE ** (-jnp.arange(ROPE_FREQS, dtype=F32) / ROPE_FREQS)
    ar = row[:, None] * inv
    ac = col[:, None] * inv
    shp = (L, 1, 1, ROPE_FREQS)
    return (jnp.cos(ar).reshape(shp).astype(dtype), jnp.sin(ar).reshape(shp).astype(dtype),
            jnp.cos(ac).reshape(shp).astype(dtype), jnp.sin(ac).reshape(shp).astype(dtype))


def _rot_half(x, cos, sin):
    x1, x2 = jnp.split(x, 2, axis=-1)
    return jnp.concatenate([x1 * cos - x2 * sin, x2 * cos + x1 * sin], -1)


def _apply_axial_rope(x, rope):
    cr, sr, cc, sc = rope
    x_row, x_col = jnp.split(x, 2, axis=-1)
    return jnp.concatenate([_rot_half(x_row, cr, sr), _rot_half(x_col, cc, sc)], -1)


def _ssd_chunked(x, dt, a, bm, cm, h0):
    b, L, g, r, p = x.shape
    n = bm.shape[-1]
    q = SSD_CHUNK
    nc = L // q
    xc = x.reshape(b, nc, q, g, r, p)
    dtc = dt.reshape(b, nc, q, g, r).astype(F32)
    bc = bm.reshape(b, nc, q, g, n)
    cc = cm.reshape(b, nc, q, g, n)
    la = jnp.cumsum(dtc * a, axis=2)
    seg = la[:, :, :, None] - la[:, :, None, :]
    tri = jnp.tril(jnp.ones((q, q), bool))[:, :, None, None]
    decay = jnp.exp(jnp.where(tri, seg, -jnp.inf))
    cb = jnp.einsum('bclgn,bcsgn->bclsg', cc, bc).astype(F32)
    w = cb[..., None] * decay * dtc[:, :, None]
    y_diag = jnp.einsum('bclsgr,bcsgrp->bclgrp', w.astype(x.dtype), xc)
    to_end = jnp.exp(la[:, :, -1:] - la) * dtc
    states = jnp.einsum('bcsgn,bcsgr,bcsgrp->bcgrpn', bc, to_end.astype(x.dtype), xc)
    chunk_decay = jnp.exp(la[:, :, -1])

    def step(h, inp):
        dec, st = inp
        return h * dec[..., None, None] + st, h

    h_final, h_start = lax.scan(step, h0.astype(F32),
                                (jnp.moveaxis(chunk_decay, 1, 0), jnp.moveaxis(states, 1, 0).astype(F32)))
    h_start = jnp.moveaxis(h_start, 0, 1)
    y_off = jnp.einsum('bclgn,bcgrpn->bclgrp', cc, h_start.astype(x.dtype)) * jnp.exp(la)[..., None].astype(x.dtype)
    return (y_diag + y_off).reshape(b, L, g, r, p), h_final


def _ssd_prepare(xbc, dt_raw, conv_w, conv_b, dt_bias):
    B, L = xbc.shape[:2]
    xbc = jax.nn.silu(_centred_dwconv(xbc, conv_w, conv_b))
    xs, bm, cm = jnp.split(xbc, [SSD_WIDTH, SSD_WIDTH + SSD_GN], axis=-1)
    xs = xs.reshape(B, L, SSD_GROUPS, SSD_HPG, SSD_HEAD_DIM)
    bm = bm.reshape(B, L, SSD_GROUPS, SSD_STATE)
    cm = cm.reshape(B, L, SSD_GROUPS, SSD_STATE)
    dt = jax.nn.softplus(dt_raw.reshape(B, L, 2, SSD_HEADS).astype(F32) + dt_bias.astype(F32))
    dt = dt.reshape(B, L, 2, SSD_GROUPS, SSD_HPG)
    return xs, bm, cm, dt


def _ssd_bidir(lat, ctx, a_log, d_skip, need_ctx):
    xs, bm, cm, dt = lat
    xsc, bmc, cmc, dtc = ctx
    B = xs.shape[0]
    a = -jnp.exp(a_log.astype(F32)).reshape(2, SSD_GROUPS, SSD_HPG)
    h0 = jnp.zeros((B, SSD_GROUPS, SSD_HPG, SSD_HEAD_DIM, SSD_STATE), F32)
    flip = lambda t: jnp.flip(t, 1)
    yc_f, hc_f = _ssd_chunked(xsc, dtc[:, :, 0], a[0], bmc, cmc, h0)
    y_f, _ = _ssd_chunked(xs, dt[:, :, 0], a[0], bm, cm, hc_f)
    yc_b, hc_b = _ssd_chunked(flip(xsc), flip(dtc[:, :, 1]), a[1], flip(bmc), flip(cmc), h0)
    y_b, _ = _ssd_chunked(flip(xs), flip(dt[:, :, 1]), a[1], flip(bm), flip(cm), hc_b)
    dsk = d_skip.reshape(SSD_GROUPS, SSD_HPG, 1)
    y = y_f + flip(y_b) + dsk * xs
    yc = (yc_f + flip(yc_b) + dsk * xsc) if need_ctx else None
    return y, yc


def _diff_scores(q, k, v, lam):
    s = jnp.einsum('bqhmd,bkhmd->bhmqk', q, k).astype(F32) * DIFF_SCALE
    p = jax.nn.softmax(s, axis=-1)
    attn = p[:, :, 0] - lam * p[:, :, 1]
    return jnp.einsum('bhqk,bkhe->bqhe', attn.astype(v.dtype), v)


def _diff_attention(q, k, v, qc, kc, vc, lam_vecs, layer_idx, rope, need_ctx):
    B, L = q.shape[:2]
    Lc = qc.shape[1]
    shp = (DIFF_HEADS, 2, DIFF_DIM)
    q = _apply_axial_rope(q.reshape(B, L, *shp), rope)
    k = _apply_axial_rope(k.reshape(B, L, *shp), rope)
    v = v.reshape(B, L, DIFF_HEADS, 2 * DIFF_DIM)
    qc = qc.reshape(B, Lc, *shp)
    kc = kc.reshape(B, Lc, *shp)
    vc = vc.reshape(B, Lc, DIFF_HEADS, 2 * DIFF_DIM)
    lv = lam_vecs.astype(F32)
    lam_init = 0.8 - 0.6 * math.exp(-0.3 * layer_idx)
    lam = jnp.exp(jnp.sum(lv[0] * lv[1])) - jnp.exp(jnp.sum(lv[2] * lv[3])) + lam_init
    k_all = jnp.concatenate([kc, k], axis=1)
    v_all = jnp.concatenate([vc, v], axis=1)
    nb = L // Q_BLOCK
    q_blocks = jnp.moveaxis(q.reshape(B, nb, Q_BLOCK, *shp), 1, 0)
    o = lax.map(lambda qb: _diff_scores(qb, k_all, v_all, lam), q_blocks)
    o = jnp.moveaxis(o, 0, 1).reshape(B, L, DIFF_HEADS, 2 * DIFF_DIM)
    out = (_rms_norm(o) * (1.0 - lam_init)).reshape(B, L, DIFF_WIDTH)
    out_c = None
    if need_ctx:
        oc = _diff_scores(qc, kc, vc, lam)
        out_c = (_rms_norm(oc) * (1.0 - lam_init)).reshape(B, Lc, DIFF_WIDTH)
    return out, out_c


def _hyena_kernels_fft(L, filt):
    w1, b1, w2, b2, w3, b3, freq = [t.astype(F32) for t in filt]
    t = jnp.linspace(0.0, 1.0, L, dtype=F32)[:, None]
    wpos = (2.0 * math.pi / L) * jnp.arange(L, dtype=F32)[:, None]
    bands = jnp.linspace(1e-4, HY_BANDS - 1, HY_BANDS, dtype=F32)
    feats = jnp.concatenate([t, jnp.cos(wpos * bands), -jnp.sin(wpos * bands)], -1)
    hdn = jnp.sin(freq * (feats @ w1 + b1))
    hdn = jnp.sin(freq * (hdn @ w2 + b2))
    hf = (hdn @ w3 + b3).reshape(L, 2, HY_ORDER, HY_WIDTH)
    deltas = jnp.abs(jnp.linspace(HY_MIN_DECAY, HY_MAX_DECAY, HY_WIDTH, dtype=F32))
    hf = hf * jnp.exp(-t * deltas)[:, None, None, :]
    kern = jnp.concatenate([hf[:, 0], jnp.zeros((1, HY_ORDER, HY_WIDTH), F32), jnp.flip(hf[1:, 1], 0)], 0)
    kern = kern * lax.rsqrt(jnp.sum(kern * kern, 0, keepdims=True) + RMS_EPS)
    return jnp.fft.rfft(kern, axis=0)


def _fft_long_conv(u, kf, bias):
    L = u.shape[1]
    uf = jnp.fft.rfft(u.astype(F32), n=2 * L, axis=1)
    y = jnp.fft.irfft(uf * kf, n=2 * L, axis=1)[:, :L]
    return (y + bias.astype(F32) * u.astype(F32)).astype(u.dtype)


def _hyena(proj, conv_w, conv_b, filt, bias):
    L = proj.shape[1]
    v, x1, x2 = jnp.split(_centred_dwconv(proj, conv_w, conv_b), 3, axis=-1)
    kf = _hyena_kernels_fft(L, filt)
    z = x1 * _fft_long_conv(v, kf[:, 0], bias[0])
    return x2 * _fft_long_conv(z, kf[:, 1], bias[1])


def _trunk_layer(h, hc, mod, mod_c, w_in, ssd_conv_w, ssd_conv_b, dt_bias, a_log, d_skip, ssd_norm_w,
                 lam_vecs, hy_conv_w, hy_conv_b, hy_filt, hy_bias, w_out, ln_g, ln_b,
                 layer_idx, rope, need_ctx):
    B, L = h.shape[:2]
    Lc = hc.shape[1]
    shift, scale, gate = jnp.split(mod[:, None, :], 3, axis=-1)
    shift_c, scale_c, gate_c = jnp.split(mod_c, 3, axis=-1)
    u = _layer_norm(h) * (1.0 + scale) + shift
    uc = _layer_norm(hc) * (1.0 + scale_c) + shift_c
    z, xbc, dtr, q, k, v, gd, hp, gh = jnp.split(u @ w_in, IN_SPLITS, axis=-1)
    zc, xbcc, dtrc, qc, kc, vc, gdc, hpc, ghc = jnp.split(uc @ w_in, IN_SPLITS, axis=-1)

    ssd_lat = _ssd_prepare(xbc, dtr, ssd_conv_w, ssd_conv_b, dt_bias)
    ssd_ctx = _ssd_prepare(xbcc, dtrc, ssd_conv_w, ssd_conv_b, dt_bias)
    y_ssd, yc_ssd = _ssd_bidir(ssd_lat, ssd_ctx, a_log, d_skip, need_ctx)
    o_ssd = _rms_norm(y_ssd.reshape(B, L, SSD_WIDTH) * jax.nn.silu(z)) * ssd_norm_w
    o_diff, oc_diff = _diff_attention(q, k, v, qc, kc, vc, lam_vecs, layer_idx, rope, need_ctx)
    o_diff = o_diff * jax.nn.silu(gd)
    o_hy = _hyena(hp, hy_conv_w, hy_conv_b, hy_filt, hy_bias) * jax.nn.silu(gh)
    o = jnp.concatenate([o_ssd, o_diff, o_hy], axis=-1) @ w_out
    h_new = _layer_norm(ALPHA * h + gate * o) * ln_g + ln_b
    if not need_ctx:
        return h_new, None
    oc_ssd = _rms_norm(yc_ssd.reshape(B, Lc, SSD_WIDTH) * jax.nn.silu(zc)) * ssd_norm_w
    oc_diff = oc_diff * jax.nn.silu(gdc)
    oc_hy = _hyena(hpc, hy_conv_w, hy_conv_b, hy_filt, hy_bias) * jax.nn.silu(ghc)
    oc = jnp.concatenate([oc_ssd, oc_diff, oc_hy], axis=-1) @ w_out
    hc_new = _layer_norm(ALPHA * hc + gate_c * oc) * ln_g + ln_b
    return h_new, hc_new


def setup_inputs(seed: int = 0) -> dict:
    key = jax.random.key(seed)
    ks = jax.random.split(key, 32)
    nrm = lambda kk, shape, s: jax.random.normal(kk, shape, F32) * s
    dt0 = jnp.exp(jax.random.uniform(ks[9], (DEPTH, 2, SSD_HEADS), F32, math.log(1e-3), math.log(1e-1)))
    return {
        "x": nrm(ks[0], (BATCH, SEQ, D_MODEL), 1.0),
        "c": nrm(ks[1], (BATCH, D_MODEL), 1.0),
        "ctx": nrm(ks[2], (BATCH, CTX_LEN, D_MODEL), 1.0),
        "c_ctx": nrm(ks[3], (D_MODEL,), 1.0),
        "w_ada": nrm(ks[4], (DEPTH, D_MODEL, 3 * D_MODEL), 0.5 * D_MODEL ** -0.5),
        "b_ada": nrm(ks[5], (DEPTH, 3 * D_MODEL), 0.02),
        "w_in": nrm(ks[6], (DEPTH, D_MODEL, N_IN), D_MODEL ** -0.5),
        "ssd_conv_w": nrm(ks[7], (DEPTH, SHORT_CONV, SSD_XBC), SHORT_CONV ** -0.5),
        "ssd_conv_b": nrm(ks[8], (DEPTH, SSD_XBC), 0.02),
        "ssd_dt_bias": dt0 + jnp.log(-jnp.expm1(-dt0)),
        "ssd_a_log": jnp.log(jax.random.uniform(ks[10], (DEPTH, 2, SSD_HEADS), F32, 1.0, 16.0)),
        "ssd_d": 1.0 + nrm(ks[11], (DEPTH, SSD_HEADS), 0.1),
        "ssd_norm_w": 1.0 + nrm(ks[12], (DEPTH, SSD_WIDTH), 0.02),
        "diff_lambda": nrm(ks[13], (DEPTH, 4, DIFF_DIM), 0.1),
        "hy_conv_w": nrm(ks[14], (DEPTH, SHORT_CONV, 3 * HY_WIDTH), SHORT_CONV ** -0.5),
        "hy_conv_b": nrm(ks[15], (DEPTH, 3 * HY_WIDTH), 0.02),
        "hy_w1": nrm(ks[16], (DEPTH, HY_EMB, HY_HIDDEN), HY_EMB ** -0.5),
        "hy_b1": nrm(ks[17], (DEPTH, HY_HIDDEN), 0.1),
        "hy_w2": nrm(ks[18], (DEPTH, HY_HIDDEN, HY_HIDDEN), HY_HIDDEN ** -0.5),
        "hy_b2": nrm(ks[19], (DEPTH, HY_HIDDEN), 0.1),
        "hy_w3": nrm(ks[20], (DEPTH, HY_HIDDEN, 2 * HY_ORDER * HY_WIDTH), HY_HIDDEN ** -0.5),
        "hy_b3": nrm(ks[21], (DEPTH, 2 * HY_ORDER * HY_WIDTH), 0.02),
        "hy_freq": 1.0 + nrm(ks[22], (DEPTH, HY_HIDDEN), 0.1),
        "hy_bias": nrm(ks[23], (DEPTH, HY_ORDER, HY_WIDTH), 0.5),
        "w_out": nrm(ks[24], (DEPTH, D_MIX, D_MODEL), BETA * D_MIX ** -0.5),
        "ln_g": 1.0 + nrm(ks[25], (DEPTH, D_MODEL), 0.02),
        "ln_b": nrm(ks[26], (DEPTH, D_MODEL), 0.02),
    }


def reference(x, c, ctx, c_ctx, w_ada, b_ada, w_in, ssd_conv_w, ssd_conv_b, ssd_dt_bias, ssd_a_log, ssd_d,
              ssd_norm_w, diff_lambda, hy_conv_w, hy_conv_b, hy_w1, hy_b1, hy_w2, hy_b2, hy_w3, hy_b3,
              hy_freq, hy_bias, w_out, ln_g, ln_b):
    h, hc = x, ctx
    rope = _axial_rope_tables(x.shape[1], x.dtype)
    sc = jax.nn.silu(c)
    scc = jax.nn.silu(c_ctx)
    for l in range(DEPTH):
        mod = sc @ w_ada[l] + b_ada[l]
        mod_c = scc @ w_ada[l] + b_ada[l]
        hy_filt = (hy_w1[l], hy_b1[l], hy_w2[l], hy_b2[l], hy_w3[l], hy_b3[l], hy_freq[l])
        h, hc = _trunk_layer(h, hc, mod, mod_c, w_in[l], ssd_conv_w[l], ssd_conv_b[l], ssd_dt_bias[l],
                             ssd_a_log[l], ssd_d[l], ssd_norm_w[l], diff_lambda[l], hy_conv_w[l],
                             hy_conv_b[l], hy_filt, hy_bias[l], w_out[l], ln_g[l], ln_b[l],
                             l, rope, l < DEPTH - 1)
    return h
```

```python
import functools
import math

import numpy as np
import jax
import jax.numpy as jnp
from jax import lax
from jax.experimental import pallas as pl
from jax.experimental.pallas import tpu as pltpu

F32 = jnp.float32
BF16 = jnp.bfloat16

D_MODEL = 1024
GRID_W = 64
SSD_HEADS = 16
SSD_HEAD_DIM = 64
SSD_WIDTH = SSD_HEADS * SSD_HEAD_DIM
SSD_GROUPS = 4
SSD_HPG = SSD_HEADS // SSD_GROUPS
SSD_STATE = 128
SSD_GN = SSD_GROUPS * SSD_STATE
SSD_XBC = SSD_WIDTH + 2 * SSD_GN
SSD_DT = 2 * SSD_HEADS
SSD_CHUNK = 128
GROUP_COLS = SSD_HPG * SSD_HEAD_DIM
DIFF_HEADS = 4
DIFF_DIM = 64
DIFF_WIDTH = DIFF_HEADS * 2 * DIFF_DIM
DIFF_SCALE = DIFF_DIM ** -0.5
ROPE_BASE = 10000.0
ROPE_FREQS = DIFF_DIM // 4
HY_WIDTH = 512
HY_ORDER = 2
HY_BANDS = 16
HY_EMB = 1 + 2 * HY_BANDS
HY_HIDDEN = 64
HY_MIN_DECAY = math.log(1e-2) / 1.5
HY_MAX_DECAY = math.log(1e-2) / 0.3
IN_SIZES = (SSD_WIDTH, SSD_XBC, SSD_DT, DIFF_WIDTH, DIFF_WIDTH, DIFF_WIDTH, DIFF_WIDTH, 3 * HY_WIDTH, HY_WIDTH)
IN_OFFS = tuple(int(s) for s in np.cumsum((0,) + IN_SIZES))
LN_EPS = 1e-5
RMS_EPS = 1e-6

LANES = 128
TM = 256
DFT2 = 64
SUB = 8
HY_CB = 256
VMEM_BIG = 56 * 2 ** 20
NEG_BIG = -1e30

_W_Z, _W_XBC, _W_Q, _W_QP, _W_K, _W_KP, _W_V, _W_GD, _W_HP, _W_GH, _W_DT, _W_END = (
    0, 1024, 3072, 3584, 4096, 4608, 5120, 5632, 6144, 7680, 8192, 8320)


def _cparams(sem, vmem=None):
    return pltpu.CompilerParams(dimension_semantics=sem, vmem_limit_bytes=vmem)


def _silu(x):
    return x * (1.0 / (1.0 + jnp.exp(-x)))


def _dot(a, b):
    return jnp.dot(a, b, preferred_element_type=F32)


def _split3(x):
    x1 = x.astype(BF16)
    r1 = x - x1.astype(F32)
    x2 = r1.astype(BF16)
    x3 = (r1 - x2.astype(F32)).astype(BF16)
    return x1, x2, x3


def _dot3(a, b):
    a1 = a.astype(BF16)
    a2 = (a - a1.astype(F32)).astype(BF16)
    b1 = b.astype(BF16)
    b2 = (b - b1.astype(F32)).astype(BF16)
    return _dot(a1, b1) + (_dot(a1, b2) + _dot(a2, b1))


def _ada_kernel(c_ref, w_ref, b_ref, o_ref):
    o_ref[0] = _dot3(_silu(c_ref[...]), w_ref[0]) + b_ref[0]


def _ada_call(cc, w_ada, b_ada):
    depth, d, n3 = w_ada.shape
    rows = cc.shape[0]
    nb = n3 // d
    return pl.pallas_call(
        _ada_kernel,
        out_shape=jax.ShapeDtypeStruct((depth, rows, n3), F32),
        grid=(depth, nb),
        in_specs=[pl.BlockSpec((rows, d), lambda l, j: (0, 0)),
                  pl.BlockSpec((1, d, d), lambda l, j: (l, 0, j)),
                  pl.BlockSpec((1, 1, d), lambda l, j: (l, 0, j))],
        out_specs=pl.BlockSpec((1, rows, d), lambda l, j: (l, 0, j)),
        compiler_params=_cparams(("arbitrary", "arbitrary")),
        name="ada_mod",
    )(cc, w_ada, b_ada.reshape(depth, 1, n3))


def _inproj_kernel(h_ref, mod_ref, w_ref, dtb_ref, cos_ref, sin_ref,
                   z_ref, xbc_ref, q_ref, k_ref, v_ref, gd_ref, hp_ref, gh_ref, dt_ref):
    nb, tm, d = h_ref.shape
    us = []
    for j in range(nb):
        x = h_ref[j]
        xc = x - jnp.mean(x, axis=-1, keepdims=True)
        var = jnp.mean(xc * xc, axis=-1, keepdims=True)
        xn = xc * lax.rsqrt(var + LN_EPS)
        shift = mod_ref[j, 0, :, 0:d]
        scale = mod_ref[j, 0, :, d:2 * d]
        us.append((xn * (1.0 + scale) + shift).astype(BF16))
    u = jnp.concatenate(us, axis=0) if nb > 1 else us[0]

    def mm(c0, c1):
        return _dot(u, w_ref[:, c0:c1])

    def put(ref, val):
        for j in range(nb):
            ref[j] = val[j * tm:(j + 1) * tm].astype(ref.dtype)

    put(z_ref, mm(_W_Z, _W_XBC))
    put(xbc_ref, mm(_W_XBC, _W_Q))
    reps = DIFF_WIDTH // LANES
    cos = jnp.concatenate([cos_ref[...]] * reps, axis=1)
    sin = jnp.concatenate([sin_ref[...]] * reps, axis=1)
    if nb > 1:
        cos = jnp.concatenate([cos] * nb, axis=0)
        sin = jnp.concatenate([sin] * nb, axis=0)
    put(q_ref, mm(_W_Q, _W_QP) * cos + mm(_W_QP, _W_K) * sin)
    put(k_ref, mm(_W_K, _W_KP) * cos + mm(_W_KP, _W_V) * sin)
    put(v_ref, mm(_W_V, _W_GD))
    put(gd_ref, mm(_W_GD, _W_HP))
    put(hp_ref, mm(_W_HP, _W_GH))
    put(gh_ref, mm(_W_GH, _W_DT))
    dtx = mm(_W_DT, _W_END) + dtb_ref[...]
    put(dt_ref, jnp.maximum(dtx, 0.0) + jnp.log1p(jnp.exp(-jnp.abs(dtx))))


def _inproj_call(hcat, modcat, w_cat, dtb, rope_cos, rope_sin, n_lat_tiles):
    B, T, D = hcat.shape
    nb = 2 if B % 2 == 0 else 1
    nt = T // TM

    def kind(i):
        return jnp.where(i >= n_lat_tiles, 1, 0)

    widths = (SSD_WIDTH, SSD_XBC, DIFF_WIDTH, DIFF_WIDTH, DIFF_WIDTH, DIFF_WIDTH, 3 * HY_WIDTH, HY_WIDTH, LANES)
    dtypes = (BF16,) * 8 + (F32,)
    out_shape = tuple(jax.ShapeDtypeStruct((B, T, w), dt) for w, dt in zip(widths, dtypes))
    out_specs = tuple(pl.BlockSpec((nb, TM, w), lambda b, i: (b, i, 0)) for w in widths)
    return pl.pallas_call(
        _inproj_kernel,
        out_shape=out_shape,
        grid=(B // nb, nt),
        in_specs=[pl.BlockSpec((nb, TM, D), lambda b, i: (b, i, 0)),
                  pl.BlockSpec((nb, 1, 1, 3 * D), lambda b, i: (b, kind(i), 0, 0)),
                  pl.BlockSpec((D, _W_END), lambda b, i: (0, 0), pipeline_mode=pl.Buffered(1)),
                  pl.BlockSpec((1, LANES), lambda b, i: (0, 0)),
                  pl.BlockSpec((TM, LANES), lambda b, i: (i, 0)),
                  pl.BlockSpec((TM, LANES), lambda b, i: (i, 0))],
        out_specs=out_specs,
        compiler_params=_cparams(("arbitrary", "arbitrary"), VMEM_BIG),
        name="ln_inproj",
    )(hcat, modcat, w_cat, dtb, rope_cos, rope_sin)


def _shortconv_kernel(x_ref, w_ref, b_ref, o_ref, *, n_lat_chunks, n_chunks, act):
    q = SSD_CHUNK
    pack = 16
    w0 = w_ref[0:1, :]
    w1 = w_ref[1:2, :]
    w2 = w_ref[2:3, :]
    bias = b_ref[...]
    rowid = lax.broadcasted_iota(jnp.int32, (q, 1), 0)

    def body(c, carry):
        r0 = pl.multiple_of(c * q, q)
        cur = x_ref[0, pl.ds(r0, q), :].astype(F32)
        p0 = pl.multiple_of(jnp.maximum(r0 - pack, 0), pack)
        n0 = pl.multiple_of(jnp.minimum(r0 + q, (n_chunks - 1) * q), pack)
        prow = x_ref[0, pl.ds(p0, pack), :].astype(F32)[pack - 1:pack, :]
        nrow = x_ref[0, pl.ds(n0, pack), :].astype(F32)[0:1, :]
        pvalid = jnp.logical_and(c != 0, c != n_lat_chunks)
        nvalid = jnp.logical_and(c != n_lat_chunks - 1, c != n_chunks - 1)
        prow = jnp.where(pvalid, prow, 0.0)
        nrow = jnp.where(nvalid, nrow, 0.0)
        prev = jnp.where(rowid == 0, prow, pltpu.roll(cur, 1, 0))
        nxt = jnp.where(rowid == q - 1, nrow, pltpu.roll(cur, q - 1, 0))
        y = prev * w0 + cur * w1 + nxt * w2 + bias
        if act:
            y = _silu(y)
        o_ref[0, pl.ds(r0, q), :] = y.astype(o_ref.dtype)
        return carry

    lax.fori_loop(0, n_chunks, body, 0)


def _shortconv_call(x, w, b, n_lat_chunks, act, out_dtype):
    B, T, C = x.shape
    cb = 256
    kern = functools.partial(_shortconv_kernel, n_lat_chunks=n_lat_chunks, n_chunks=T // SSD_CHUNK, act=act)
    return pl.pallas_call(
        kern,
        out_shape=jax.ShapeDtypeStruct((B, T, C), out_dtype),
        grid=(B, C // cb),
        in_specs=[pl.BlockSpec((1, T, cb), lambda b, j: (b, 0, j)),
                  pl.BlockSpec((3, cb), lambda b, j: (0, j)),
                  pl.BlockSpec((1, cb), lambda b, j: (0, j))],
        out_specs=pl.BlockSpec((1, T, cb), lambda b, j: (b, 0, j)),
        compiler_params=_cparams(("arbitrary", "arbitrary")),
        name="shortconv_act" if act else "shortconv",
    )(x, w, b.reshape(1, C))


def _ssd_kernel(x_ref, b_ref, c_ref, z_ref, dtc_ref, dtr_ref, alr_ref, alc_ref, dsk_ref, o_ref,
                yacc, hst, *, n_lat_chunks, n_ctx_chunks):
    q = SSD_CHUNK
    r_heads = SSD_HPG
    gc = GROUP_COLS
    nch = n_lat_chunks + n_ctx_chunks
    li = lax.broadcasted_iota(jnp.int32, (q, q), 0)
    si = lax.broadcasted_iota(jnp.int32, (q, q), 1)
    low = si <= li
    upp = si >= li
    tri_l = jnp.where(low, 1.0, 0.0).astype(BF16)
    tri_u = jnp.where(upp, 1.0, 0.0).astype(BF16)
    a_row = -jnp.exp(alr_ref[0])
    a_col = -jnp.exp(alc_ref[0])
    dsk = dsk_ref[0]
    lane_head = lax.broadcasted_iota(jnp.int32, (1, gc), 1) // SSD_HEAD_DIM

    def cum_left(tri, x):
        x1, x2, x3 = _split3(x)
        return _dot(tri, x1) + (_dot(tri, x2) + _dot(tri, x3))

    def cum_right(x, tri):
        x1, x2, x3 = _split3(x)
        return _dot(x1, tri) + (_dot(x2, tri) + _dot(x3, tri))

    def expand(cols, c0):
        out = cols[:, c0 + r_heads - 1:c0 + r_heads]
        for r in range(r_heads - 2, -1, -1):
            out = jnp.where(lane_head == r, cols[:, c0 + r:c0 + r + 1], out)
        return out

    def load(c):
        r0 = pl.multiple_of(c * q, q)
        xc = x_ref[0, pl.ds(r0, q), :].astype(F32)
        bc = b_ref[0, pl.ds(r0, q), :]
        cc = c_ref[0, pl.ds(r0, q), :]
        dtc = dtc_ref[0, 0, pl.ds(r0, q), :]
        return r0, xc, bc, cc, dtc

    def fwd_body(j, carry):
        c = jnp.where(j < n_ctx_chunks, n_lat_chunks + j, j - n_ctx_chunks)
        r0, xc, bc, cc, dtc = load(c)
        dtr = dtr_ref[0, 0, c]
        dta_c = dtc * a_row
        dta_r = dtr * a_col
        la_cf = cum_left(tri_l, dta_c)
        la_cb = cum_left(tri_u, dta_c)
        la_rf = cum_right(dta_r, tri_u)
        la_rb = cum_right(dta_r, tri_l)
        cb = lax.dot_general(cc, bc, (((1,), (1,)), ((), ())), preferred_element_type=F32)
        xf = xc * expand(dtc, 0)
        xb = xc * expand(dtc, r_heads)
        y = dsk * xc
        for r in range(r_heads):
            seg_f = la_cf[:, r:r + 1] - la_rf[r:r + 1, :]
            seg_b = la_cb[:, r_heads + r:r_heads + r + 1] - la_rb[r_heads + r:r_heads + r + 1, :]
            wf = (jnp.exp(jnp.where(low, seg_f, NEG_BIG)) * cb).astype(BF16)
            wb = (jnp.exp(jnp.where(upp, seg_b, NEG_BIG)) * cb).astype(BF16)
            sel = lane_head == r
            y = y + _dot(wf, jnp.where(sel, xf, 0.0).astype(BF16))
            y = y + _dot(wb, jnp.where(sel, xb, 0.0).astype(BF16))
        h = hst[...]
        y = y + _dot(cc, h.astype(BF16)) * expand(jnp.exp(la_cf), 0)
        yacc[pl.ds(r0, q), :] = y
        last = la_cf[q - 1:q, :]
        xs = (xf * expand(jnp.exp(last - la_cf), 0)).astype(BF16)
        s_new = lax.dot_general(bc, xs, (((0,), (0,)), ((), ())), preferred_element_type=F32)
        hst[...] = h * expand(jnp.exp(last), 0) + s_new
        return carry

    hst[...] = jnp.zeros_like(hst)
    lax.fori_loop(0, nch, fwd_body, 0)

    def bwd_body(j, carry):
        c = nch - 1 - j
        r0, xc, bc, cc, dtc = load(c)
        la_cb = cum_left(tri_u, dtc * a_row)
        h = hst[...]
        y = yacc[pl.ds(r0, q), :] + _dot(cc, h.astype(BF16)) * expand(jnp.exp(la_cb), r_heads)
        first = la_cb[0:1, :]
        xs = (xc * expand(dtc * jnp.exp(first - la_cb), r_heads)).astype(BF16)
        s_new = lax.dot_general(bc, xs, (((0,), (0,)), ((), ())), preferred_element_type=F32)
        hst[...] = h * expand(jnp.exp(first), r_heads) + s_new
        zz = z_ref[0, pl.ds(r0, q), :].astype(F32)
        o_ref[0, pl.ds(r0, q), :] = (y * _silu(zz)).astype(o_ref.dtype)
        return carry

    hst[...] = jnp.zeros_like(hst)
    lax.fori_loop(0, nch, bwd_body, 0)


def _ssd_call(xbc_c, z, dt_col, dt_row, alog_row, alog_col, dsk, n_lat_chunks, n_ctx_chunks):
    B, T, _ = xbc_c.shape
    G = SSD_GROUPS
    gc = GROUP_COLS
    nch = T // SSD_CHUNK
    b_off = SSD_WIDTH // SSD_STATE
    c_off = (SSD_WIDTH + SSD_GN) // SSD_STATE
    kern = functools.partial(_ssd_kernel, n_lat_chunks=n_lat_chunks, n_ctx_chunks=n_ctx_chunks)
    return pl.pallas_call(
        kern,
        out_shape=jax.ShapeDtypeStruct((B, T, SSD_WIDTH), BF16),
        grid=(B, G),
        in_specs=[pl.BlockSpec((1, T, gc), lambda b, g: (b, 0, g)),
                  pl.BlockSpec((1, T, SSD_STATE), lambda b, g: (b, 0, b_off + g)),
                  pl.BlockSpec((1, T, SSD_STATE), lambda b, g: (b, 0, c_off + g)),
                  pl.BlockSpec((1, T, gc), lambda b, g: (b, 0, g)),
                  pl.BlockSpec((1, 1, T, 2 * SSD_HPG), lambda b, g: (b, g, 0, 0)),
                  pl.BlockSpec((1, 1, nch, 2 * SSD_HPG, SSD_CHUNK), lambda b, g: (b, g, 0, 0, 0)),
                  pl.BlockSpec((1, 1, 2 * SSD_HPG), lambda b, g: (g, 0, 0)),
                  pl.BlockSpec((1, 2 * SSD_HPG, 1), lambda b, g: (g, 0, 0)),
                  pl.BlockSpec((1, 1, gc), lambda b, g: (g, 0, 0))],
        out_specs=pl.BlockSpec((1, T, gc), lambda b, g: (b, 0, g)),
        scratch_shapes=[pltpu.VMEM((T, gc), F32), pltpu.VMEM((SSD_STATE, gc), F32)],
        compiler_params=_cparams(("arbitrary", "arbitrary"), VMEM_BIG),
        name="ssd",
    )(xbc_c, xbc_c, xbc_c, z, dt_col, dt_row, alog_row, alog_col, dsk)


def _attn_kernel(lam_ref, q_ref, k_ref, v_ref, gd_ref, o_ref, m_sc, l_sc, acc_sc,
                 *, n_lat_tiles, n_kv_tiles, lam_init):
    tq = q_ref.shape[1]
    tk = TM
    qi = pl.program_id(2)
    lane = lax.broadcasted_iota(jnp.int32, (1, 2 * DIFF_DIM), 1)
    qt = q_ref[0]
    zero = jnp.zeros_like(qt)
    qmaps = (jnp.where(lane < DIFF_DIM, qt, zero), jnp.where(lane >= DIFF_DIM, qt, zero))
    m_sc[...] = jnp.full(m_sc.shape, NEG_BIG, F32)
    l_sc[...] = jnp.zeros_like(l_sc)
    acc_sc[...] = jnp.zeros_like(acc_sc)

    def body(t, carry):
        k0 = pl.multiple_of(t * tk, tk)
        kt = k_ref[0, pl.ds(k0, tk), :]
        vt = v_ref[0, pl.ds(k0, tk), :]
        for m in range(2):
            s = lax.dot_general(qmaps[m], kt, (((1,), (1,)), ((), ())), preferred_element_type=F32)
            m_old = m_sc[m]
            m_new = jnp.maximum(m_old, jnp.max(s, axis=1, keepdims=True))
            alpha = jnp.exp(m_old - m_new)
            p = jnp.exp(s - jnp.concatenate([m_new] * (tk // LANES), axis=1))
            l_sc[m] = alpha * l_sc[m] + jnp.sum(p, axis=1, keepdims=True)
            acc_sc[m] = alpha * acc_sc[m] + _dot(p.astype(BF16), vt)
            m_sc[m] = m_new
        return carry

    start = jnp.where(qi < n_lat_tiles, 0, n_lat_tiles)
    lax.fori_loop(start, n_kv_tiles, body, 0)

    lv = lam_ref[...]
    lam = (jnp.exp(jnp.sum(lv[0:1] * lv[1:2], axis=1, keepdims=True))
           - jnp.exp(jnp.sum(lv[2:3] * lv[3:4], axis=1, keepdims=True)) + lam_init)
    o = acc_sc[0] / l_sc[0] - lam * (acc_sc[1] / l_sc[1])
    o = o * lax.rsqrt(jnp.mean(o * o, axis=1, keepdims=True) + RMS_EPS) * (1.0 - lam_init)
    o_ref[0] = (o * _silu(gd_ref[0].astype(F32))).astype(o_ref.dtype)


def _attn_call(lam_vecs, q, k, v, gd, n_lat_tiles, n_q_tiles, lam_init):
    B, T, _ = q.shape
    hw = 2 * DIFF_DIM
    kern = functools.partial(_attn_kernel, n_lat_tiles=n_lat_tiles, n_kv_tiles=T // TM, lam_init=lam_init)
    return pl.pallas_call(
        kern,
        out_shape=jax.ShapeDtypeStruct((B, T, DIFF_WIDTH), BF16),
        grid=(B, DIFF_HEADS, n_q_tiles),
        in_specs=[pl.BlockSpec((4, DIFF_DIM), lambda b, h, i: (0, 0)),
                  pl.BlockSpec((1, TM, hw), lambda b, h, i: (b, i, h)),
                  pl.BlockSpec((1, T, hw), lambda b, h, i: (b, 0, h)),
                  pl.BlockSpec((1, T, hw), lambda b, h, i: (b, 0, h)),
                  pl.BlockSpec((1, TM, hw), lambda b, h, i: (b, i, h))],
        out_specs=pl.BlockSpec((1, TM, hw), lambda b, h, i: (b, i, h)),
        scratch_shapes=[pltpu.VMEM((2, TM, LANES), F32), pltpu.VMEM((2, TM, LANES), F32),
                        pltpu.VMEM((2, TM, hw), F32)],
        compiler_params=_cparams(("arbitrary", "arbitrary", "arbitrary")),
        name="diff_attn",
    )(lam_vecs, q, k, v, gd)


def _hyfilt_kernel(feat_ref, t_ref, w1_ref, b1_ref, w2_ref, b2_ref, fr_ref, w3a_ref, b3a_ref,
                   w3b_ref, b3b_ref, dl_ref, oa_ref, ob_ref):
    fr = fr_ref[...]
    h1 = jnp.sin(fr * (_dot3(feat_ref[...], w1_ref[...]) + b1_ref[...]))
    h2 = jnp.sin(fr * (_dot3(h1, w2_ref[...]) + b2_ref[...]))
    win = jnp.exp(-t_ref[...] * dl_ref[...])
    ca = (_dot3(h2, w3a_ref[...]) + b3a_ref[...]) * win
    cb = (_dot3(h2, w3b_ref[...]) + b3b_ref[...]) * win
    rowid = lax.broadcasted_iota(jnp.int32, (ca.shape[0], 1), 0)
    cb = jnp.where(rowid == 0, 0.0, cb)
    energy = jnp.sum(ca * ca, axis=0, keepdims=True) + jnp.sum(cb * cb, axis=0, keepdims=True)
    sc = lax.rsqrt(energy + RMS_EPS)
    oa_ref[0] = ca * sc
    ob_ref[0] = cb * sc


def _hyfilt_call(feats, tpos, w1p, b1, w2, b2, freq, w3, b3, deltas):
    L = feats.shape[0]
    cb = HY_CB
    ncb = HY_WIDTH // cb
    hid = HY_HIDDEN

    def full(shape):
        return pl.BlockSpec(shape, lambda o, j: (0,) * len(shape))

    def w3spec(d):
        return pl.BlockSpec((hid, cb), lambda o, j: (0, (d * HY_ORDER + o) * ncb + j))

    def b3spec(d):
        return pl.BlockSpec((1, cb), lambda o, j: (0, (d * HY_ORDER + o) * ncb + j))

    out = jax.ShapeDtypeStruct((HY_ORDER, L, HY_WIDTH), F32)
    ospec = pl.BlockSpec((1, L, cb), lambda o, j: (o, 0, j))
    return pl.pallas_call(
        _hyfilt_kernel,
        out_shape=(out, out),
        grid=(HY_ORDER, ncb),
        in_specs=[full((L, hid)), full((L, 1)), full((hid, hid)), full((1, hid)), full((hid, hid)),
                  full((1, hid)), full((1, hid)), w3spec(0), b3spec(0), w3spec(1), b3spec(1),
                  pl.BlockSpec((1, cb), lambda o, j: (0, j))],
        out_specs=(ospec, ospec),
        compiler_params=_cparams(("arbitrary", "arbitrary"), VMEM_BIG),
        name="hy_filter",
    )(feats, tpos, w1p, b1, w2, b2, freq, w3, b3, w3, b3, deltas)


def _twiddle(twc_ref, tws_ref, k1, width):
    reps = width // LANES
    c = jnp.concatenate([twc_ref[k1]] * reps, axis=1)
    s = jnp.concatenate([tws_ref[k1]] * reps, axis=1)
    return c, s


def _hyspec_kernel(fa_ref, fb_ref, gl1_ref, twc_ref, tws_ref, f64_ref, kr_ref, ki_ref, wsr, wsi, *, k1c):
    kc = pl.program_id(2)
    cw = fa_ref.shape[-1]
    gl1 = gl1_ref[0]
    for j in range(DFT2 // SUB):
        xa = fa_ref[0, :, SUB * j:SUB * (j + 1), :]
        xb = fb_ref[0, :, SUB * j:SUB * (j + 1), :]
        h = xa.shape[0]
        xs = jnp.concatenate([xa.reshape(h * SUB, cw), xb.reshape(h * SUB, cw)], axis=1).astype(BF16)
        a = _dot(gl1, xs)
        wsr[:, SUB * j:SUB * (j + 1), :] = a[:k1c * SUB].reshape(k1c, SUB, 2 * cw)
        wsi[:, SUB * j:SUB * (j + 1), :] = a[k1c * SUB:].reshape(k1c, SUB, 2 * cw)
    f64 = f64_ref[...]

    def body(k, carry):
        c, s = _twiddle(twc_ref, tws_ref, kc * k1c + k, 2 * cw)
        ar = wsr[k]
        ai = wsi[k]
        br = ar * c + ai * s
        bi = ai * c - ar * s
        x = _dot(f64, jnp.concatenate([br, bi], axis=0).astype(BF16))
        xr = x[:DFT2]
        xi = x[DFT2:]
        r0 = pl.multiple_of(k * DFT2, DFT2)
        kr_ref[0, pl.ds(r0, DFT2), :] = (xr[:, :cw] + xr[:, cw:]).astype(kr_ref.dtype)
        ki_ref[0, pl.ds(r0, DFT2), :] = (xi[:, :cw] - xi[:, cw:]).astype(ki_ref.dtype)
        return carry

    lax.fori_loop(0, k1c, body, 0)


def _hyspec_call(fa, fb, tabs):
    _, L, _ = fa.shape
    H = L // DFT2
    n1 = 2 * H
    k1c = tabs["k1c"]
    nkc = n1 // k1c
    cb = HY_CB
    ncb = HY_WIDTH // cb
    fa4 = fa.reshape(HY_ORDER, H, DFT2, HY_WIDTH)
    fb4 = fb.reshape(HY_ORDER, H, DFT2, HY_WIDTH)
    out = jax.ShapeDtypeStruct((HY_ORDER, n1 * DFT2, HY_WIDTH), BF16)
    ospec = pl.BlockSpec((1, k1c * DFT2, cb), lambda o, j, kc: (o, kc, j))
    fspec = pl.BlockSpec((1, H, DFT2, cb), lambda o, j, kc: (o, 0, 0, j))
    kern = functools.partial(_hyspec_kernel, k1c=k1c)
    return pl.pallas_call(
        kern,
        out_shape=(out, out),
        grid=(HY_ORDER, ncb, nkc),
        in_specs=[fspec, fspec,
                  pl.BlockSpec((1, 2 * k1c * SUB, H * SUB), lambda o, j, kc: (kc, 0, 0)),
                  pl.BlockSpec((n1, DFT2, LANES), lambda o, j, kc: (0, 0, 0), pipeline_mode=pl.Buffered(1)),
                  pl.BlockSpec((n1, DFT2, LANES), lambda o, j, kc: (0, 0, 0), pipeline_mode=pl.Buffered(1)),
                  pl.BlockSpec((2 * DFT2, 2 * DFT2), lambda o, j, kc: (0, 0))],
        out_specs=(ospec, ospec),
        scratch_shapes=[pltpu.VMEM((k1c, DFT2, 2 * cb), F32), pltpu.VMEM((k1c, DFT2, 2 * cb), F32)],
        compiler_params=_cparams(("arbitrary", "arbitrary", "arbitrary"), VMEM_BIG),
        name="hy_spectrum",
    )(fa4, fb4, tabs["gl1"], tabs["twc"], tabs["tws"], tabs["f64"])


def _hyconv_kernel(*refs, k1c, nkc, n_fft, final):
    u_ref, g_ref = refs[0], refs[1]
    pos = 2
    gh_ref = None
    if final:
        gh_ref = refs[pos]
        pos += 1
    kr_ref, ki_ref, bias_ref, gl1_ref, gl1i_ref, twc_ref, tws_ref, f64_ref, f64i_ref = refs[pos:pos + 9]
    o_ref, wsr, wsi, acc = refs[pos + 9:pos + 13]
    kc = pl.program_id(2)
    cw = u_ref.shape[-1]
    h = u_ref.shape[1]

    gl1 = gl1_ref[0]
    for j in range(DFT2 // SUB):
        xs = u_ref[0, :, SUB * j:SUB * (j + 1), :].reshape(h * SUB, cw).astype(BF16)
        a = _dot(gl1, xs)
        wsr[:, SUB * j:SUB * (j + 1), :] = a[:k1c * SUB].reshape(k1c, SUB, cw)
        wsi[:, SUB * j:SUB * (j + 1), :] = a[k1c * SUB:].reshape(k1c, SUB, cw)
    f64 = f64_ref[...]
    f64i = f64i_ref[...]

    def body(k, carry):
        c, s = _twiddle(twc_ref, tws_ref, kc * k1c + k, cw)
        ar = wsr[k]
        ai = wsi[k]
        br = ar * c + ai * s
        bi = ai * c - ar * s
        x = _dot(f64, jnp.concatenate([br, bi], axis=0).astype(BF16))
        xr = x[:DFT2]
        xi = x[DFT2:]
        r0 = pl.multiple_of(k * DFT2, DFT2)
        kr = kr_ref[0, pl.ds(r0, DFT2), :].astype(F32)
        ki = ki_ref[0, pl.ds(r0, DFT2), :].astype(F32)
        yr = xr * kr - xi * ki
        yi = xr * ki + xi * kr
        cc = _dot(f64i, jnp.concatenate([yr, yi], axis=0).astype(BF16))
        cr = cc[:DFT2]
        ci = cc[DFT2:]
        wsr[k] = cr * c - ci * s
        wsi[k] = ci * c + cr * s
        return carry

    lax.fori_loop(0, k1c, body, 0)

    @pl.when(kc == 0)
    def _():
        acc[...] = jnp.zeros_like(acc)

    gl1i = gl1i_ref[0]
    for j in range(DFT2 // SUB):
        dr = wsr[:, SUB * j:SUB * (j + 1), :].reshape(k1c * SUB, cw)
        di = wsi[:, SUB * j:SUB * (j + 1), :].reshape(k1c * SUB, cw)
        yj = _dot(gl1i, jnp.concatenate([dr, di], axis=0).astype(BF16))
        acc[:, SUB * j:SUB * (j + 1), :] += yj.reshape(h, SUB, cw)

    @pl.when(kc == nkc - 1)
    def _():
        u = u_ref[0]
        y = g_ref[0] * (acc[...] * (1.0 / n_fft) + bias_ref[0] * u)
        if final:
            y = y * _silu(gh_ref[0].astype(F32))
        o_ref[0] = y.astype(o_ref.dtype)


def _hyconv_call(u4, u_rb, u_cb, g4, g_rb, g_cb, gh4, gh_rb, kr, ki, bias, order, tabs, H, out_dtype):
    B = u4.shape[0]
    k1c = tabs["k1c"]
    n1 = 2 * H
    nkc = n1 // k1c
    cb = HY_CB
    final = gh4 is not None

    def seq_spec(rb, off):
        return pl.BlockSpec((1, H, DFT2, cb), lambda b, j, kc: (b, rb, 0, off + j))

    in_specs = [seq_spec(u_rb, u_cb), seq_spec(g_rb, g_cb)]
    args = [u4, g4]
    if final:
        in_specs.append(seq_spec(gh_rb, 0))
        args.append(gh4)
    in_specs += [
        pl.BlockSpec((1, k1c * DFT2, cb), lambda b, j, kc: (order, kc, j)),
        pl.BlockSpec((1, k1c * DFT2, cb), lambda b, j, kc: (order, kc, j)),
        pl.BlockSpec((1, 1, cb), lambda b, j, kc: (order, 0, j)),
        pl.BlockSpec((1, 2 * k1c * SUB, H * SUB), lambda b, j, kc: (kc, 0, 0)),
        pl.BlockSpec((1, H * SUB, 2 * k1c * SUB), lambda b, j, kc: (kc, 0, 0)),
        pl.BlockSpec((n1, DFT2, LANES), lambda b, j, kc: (0, 0, 0), pipeline_mode=pl.Buffered(1)),
        pl.BlockSpec((n1, DFT2, LANES), lambda b, j, kc: (0, 0, 0), pipeline_mode=pl.Buffered(1)),
        pl.BlockSpec((2 * DFT2, 2 * DFT2), lambda b, j, kc: (0, 0)),
        pl.BlockSpec((2 * DFT2, 2 * DFT2), lambda b, j, kc: (0, 0)),
    ]
    args += [kr, ki, bias.reshape(HY_ORDER, 1, HY_WIDTH), tabs["gl1"], tabs["gl1i"], tabs["twc"], tabs["tws"],
             tabs["f64"], tabs["f64i"]]
    kern = functools.partial(_hyconv_kernel, k1c=k1c, nkc=nkc, n_fft=float(n1 * DFT2), final=final)
    return pl.pallas_call(
        kern,
        out_shape=jax.ShapeDtypeStruct((B, H, DFT2, HY_WIDTH), out_dtype),
        grid=(B, HY_WIDTH // cb, nkc),
        in_specs=in_specs,
        out_specs=pl.BlockSpec((1, H, DFT2, cb), lambda b, j, kc: (b, 0, 0, j)),
        scratch_shapes=[pltpu.VMEM((k1c, DFT2, cb), F32), pltpu.VMEM((k1c, DFT2, cb), F32),
                        pltpu.VMEM((H, DFT2, cb), F32)],
        compiler_params=_cparams(("arbitrary", "arbitrary", "arbitrary"), VMEM_BIG),
        name="hy_conv",
    )(*args)


def _dft_tables(H):
    n1 = 2 * H
    n = n1 * DFT2
    k1c = min(n1, 32)
    nkc = n1 // k1c
    k1 = np.arange(n1, dtype=np.float64)
    th = 2.0 * np.pi * np.outer(k1, np.arange(H, dtype=np.float64)) / n1
    eye = np.eye(SUB)
    gre = np.kron(np.cos(th), eye).reshape(nkc, k1c * SUB, H * SUB)
    gim = np.kron(-np.sin(th), eye).reshape(nkc, k1c * SUB, H * SUB)
    gl1 = np.concatenate([gre, gim], axis=1)
    gire = np.kron(np.cos(th).T, eye).reshape(H * SUB, nkc, k1c * SUB).transpose(1, 0, 2)
    giim = np.kron(-np.sin(th).T, eye).reshape(H * SUB, nkc, k1c * SUB).transpose(1, 0, 2)
    gl1i = np.concatenate([gire, giim], axis=2)
    ph = 2.0 * np.pi * np.outer(k1, np.arange(DFT2, dtype=np.float64)) / n
    twc = np.repeat(np.cos(ph)[:, :, None], LANES, axis=2)
    tws = np.repeat(np.sin(ph)[:, :, None], LANES, axis=2)
    ps = 2.0 * np.pi * np.outer(np.arange(DFT2), np.arange(DFT2)) / DFT2
    c64, s64 = np.cos(ps), np.sin(ps)
    f64 = np.block([[c64, s64], [-s64, c64]])
    f64i = np.block([[c64, -s64], [s64, c64]])
    as_bf = lambda a: jnp.asarray(a, F32).astype(BF16)
    return dict(k1c=k1c, gl1=as_bf(gl1), gl1i=as_bf(gl1i), twc=jnp.asarray(twc, F32),
                tws=jnp.asarray(tws, F32), f64=as_bf(f64), f64i=as_bf(f64i))


def _hy_features(L):
    t = np.linspace(0.0, 1.0, L)[:, None]
    wpos = (2.0 * math.pi / L) * np.arange(L, dtype=np.float64)[:, None]
    bands = np.linspace(1e-4, HY_BANDS - 1, HY_BANDS)
    feats = np.concatenate([t, np.cos(wpos * bands), -np.sin(wpos * bands)], -1)
    feats = np.pad(feats, ((0, 0), (0, HY_HIDDEN - HY_EMB)))
    return jnp.asarray(feats, F32), jnp.asarray(t, F32)


def _outproj_kernel(y_ref, od_ref, oh_ref, h_ref, mod_ref, w_ref, nw_ref, g_ref, b_ref, o_ref, *, alpha):
    nb, tm, d = h_ref.shape
    sw = y_ref.shape[-1]
    dw = od_ref.shape[-1]
    ys, ods, ohs = [], [], []
    for j in range(nb):
        y = y_ref[j].astype(F32)
        y = y * lax.rsqrt(jnp.mean(y * y, axis=-1, keepdims=True) + RMS_EPS) * nw_ref[...]
        ys.append(y.astype(BF16))
        ods.append(od_ref[j])
        ohs.append(oh_ref[j])
    cat = (lambda xs: jnp.concatenate(xs, axis=0)) if nb > 1 else (lambda xs: xs[0])
    o = (_dot(cat(ys), w_ref[0:sw, :]) + _dot(cat(ods), w_ref[sw:sw + dw, :])
         + _dot(cat(ohs), w_ref[sw + dw:, :]))
    for j in range(nb):
        gate = mod_ref[j, 0, :, 2 * d:3 * d]
        r = alpha * h_ref[j] + gate * o[j * tm:(j + 1) * tm]
        rc = r - jnp.mean(r, axis=-1, keepdims=True)
        var = jnp.mean(rc * rc, axis=-1, keepdims=True)
        o_ref[j] = rc * lax.rsqrt(var + LN_EPS) * g_ref[...] + b_ref[...]


def _outproj_call(yg, od, oh, hcat, modcat, w_out, norm_w, ln_g, ln_b, n_lat_tiles, n_tiles, alpha):
    B, T, D = hcat.shape
    nb = 2 if B % 2 == 0 else 1

    def kind(i):
        return jnp.where(i >= n_lat_tiles, 1, 0)

    def tok(w):
        return pl.BlockSpec((nb, TM, w), lambda b, i: (b, i, 0))

    def vec(w):
        return pl.BlockSpec((1, w), lambda b, i: (0, 0))

    kern = functools.partial(_outproj_kernel, alpha=alpha)
    return pl.pallas_call(
        kern,
        out_shape=jax.ShapeDtypeStruct((B, n_tiles * TM, D), F32),
        grid=(B // nb, n_tiles),
        in_specs=[tok(SSD_WIDTH), tok(DIFF_WIDTH), tok(HY_WIDTH), tok(D),
                  pl.BlockSpec((nb, 1, 1, 3 * D), lambda b, i: (b, kind(i), 0, 0)),
                  pl.BlockSpec(w_out.shape, lambda b, i: (0, 0)),
                  vec(SSD_WIDTH), vec(D), vec(D)],
        out_specs=tok(D),
        compiler_params=_cparams(("arbitrary", "arbitrary"), VMEM_BIG),
        name="outproj",
    )(yg, od, oh, hcat, modcat, w_out, norm_w.reshape(1, -1), ln_g.reshape(1, -1), ln_b.reshape(1, -1))


def _rope_tables(L, Lc):
    rows = np.repeat(np.arange(L // GRID_W), GRID_W).astype(np.float64)
    cols = np.tile(np.arange(GRID_W), L // GRID_W).astype(np.float64)
    inv = ROPE_BASE ** (-np.arange(ROPE_FREQS, dtype=np.float64) / ROPE_FREQS)
    lane = np.arange(LANES)
    f = lane % ROPE_FREQS
    is_row = (lane % DIFF_DIM) < (DIFF_DIM // 2)
    ang = np.where(is_row[None, :], rows[:, None] * inv[f][None, :], cols[:, None] * inv[f][None, :])
    cos = np.concatenate([np.cos(ang), np.ones((Lc, LANES))], 0)
    sin = np.concatenate([np.sin(ang), np.zeros((Lc, LANES))], 0)
    return jnp.asarray(cos, F32), jnp.asarray(sin, F32)


def _rot_partner_cols(w):
    n = w.shape[1]
    col = np.arange(n)
    first = (col % (2 * ROPE_FREQS)) < ROPE_FREQS
    src = np.where(first, col + ROPE_FREQS, col - ROPE_FREQS)
    sign = np.where(first, -1.0, 1.0).astype(np.float32)
    return w[:, src] * sign


def _dt_perm():
    g, d, r = np.meshgrid(np.arange(SSD_GROUPS), np.arange(2), np.arange(SSD_HPG), indexing="ij")
    return (d * SSD_HEADS + g * SSD_HPG + r).reshape(-1)


def _layout_w_in(w):
    o = IN_OFFS
    z, xbc, dt = w[:, o[0]:o[1]], w[:, o[1]:o[2]], w[:, o[2]:o[3]]
    q, k, v, gd = w[:, o[3]:o[4]] * DIFF_SCALE, w[:, o[4]:o[5]], w[:, o[5]:o[6]], w[:, o[6]:o[7]]
    hp, gh = w[:, o[7]:o[8]], w[:, o[8]:o[9]]
    dtp = jnp.pad(dt[:, _dt_perm()], ((0, 0), (0, LANES - SSD_DT)))
    cat = jnp.concatenate([z, xbc, q, _rot_partner_cols(q), k, _rot_partner_cols(k), v, gd, hp, gh, dtp], axis=1)
    return cat.astype(BF16)


def kernel(x, c, ctx, c_ctx, w_ada, b_ada, w_in, ssd_conv_w, ssd_conv_b, ssd_dt_bias, ssd_a_log, ssd_d,
           ssd_norm_w, diff_lambda, hy_conv_w, hy_conv_b, hy_w1, hy_b1, hy_w2, hy_b2, hy_w3, hy_b3,
           hy_freq, hy_bias, w_out, ln_g, ln_b):
    B, L, D = x.shape
    Lc = ctx.shape[1]
    depth = w_in.shape[0]
    T = L + Lc
    assert L % TM == 0 and Lc % TM == 0 and L % GRID_W == 0 and D == D_MODEL
    n_lat_tiles, n_tiles = L // TM, T // TM
    n_lat_chunks, n_ctx_chunks = L // SSD_CHUNK, Lc // SSD_CHUNK
    nch = n_lat_chunks + n_ctx_chunks
    G, R8 = SSD_GROUPS, 2 * SSD_HPG
    alpha = (2 * depth) ** 0.25
    H_lat, H_ctx = L // DFT2, Lc // DFT2
    assert H_lat % H_ctx == 0

    rope_cos, rope_sin = _rope_tables(L, Lc)
    tabs_lat, tabs_ctx = _dft_tables(H_lat), _dft_tables(H_ctx)
    feats_lat, t_lat = _hy_features(L)
    feats_ctx, t_ctx = _hy_features(Lc)
    deltas = jnp.asarray(np.abs(np.linspace(HY_MIN_DECAY, HY_MAX_DECAY, HY_WIDTH))[None, :], F32)

    rows = -(-(B + 1) // SUB) * SUB
    cc = jnp.zeros((rows, D), F32).at[:B].set(c).at[B].set(c_ctx)
    mod_all = _ada_call(cc, w_ada, b_ada)

    hcat = jnp.concatenate([x, ctx], axis=1)
    perm = _dt_perm()
    for l in range(depth):
        need_ctx = l < depth - 1
        mod = mod_all[l]
        modcat = jnp.stack([mod[:B], jnp.broadcast_to(mod[B], (B, 3 * D))], axis=1)[:, :, None, :]
        dtb = jnp.pad(ssd_dt_bias[l].reshape(-1)[perm], (0, LANES - SSD_DT)).reshape(1, LANES)
        z, xbc, q, k, v, gd, hp, gh, dt = _inproj_call(
            hcat, modcat, _layout_w_in(w_in[l]), dtb, rope_cos, rope_sin, n_lat_tiles)

        xbc_c = _shortconv_call(xbc, ssd_conv_w[l], ssd_conv_b[l], n_lat_chunks, True, BF16)
        dt_g = dt[:, :, :SSD_DT].reshape(B, T, G, R8).transpose(0, 2, 1, 3)
        dt_row = dt_g.reshape(B, G, nch, SSD_CHUNK, R8).transpose(0, 1, 2, 4, 3)
        alog = ssd_a_log[l].reshape(2, G, SSD_HPG).transpose(1, 0, 2).reshape(G, R8)
        dsk = jnp.repeat(ssd_d[l].reshape(G, SSD_HPG), SSD_HEAD_DIM, axis=1).reshape(G, 1, GROUP_COLS)
        yg = _ssd_call(xbc_c, z, dt_g, dt_row, alog.reshape(G, 1, R8), alog.reshape(G, R8, 1), dsk,
                       n_lat_chunks, n_ctx_chunks)

        lam_init = 0.8 - 0.6 * math.exp(-0.3 * l)
        od = _attn_call(diff_lambda[l], q, k, v, gd, n_lat_tiles, n_tiles if need_ctx else n_lat_tiles, lam_init)

        hpc = _shortconv_call(hp, hy_conv_w[l], hy_conv_b[l], n_lat_chunks, False, F32)
        hp4 = hpc.reshape(B, T // DFT2, DFT2, 3 * HY_WIDTH)
        gh4 = gh.reshape(B, T // DFT2, DFT2, HY_WIDTH)
        w1p = jnp.pad(hy_w1[l], ((0, HY_HIDDEN - HY_EMB), (0, 0)))
        filt_args = (w1p, hy_b1[l].reshape(1, -1), hy_w2[l], hy_b2[l].reshape(1, -1), hy_freq[l].reshape(1, -1),
                     hy_w3[l], hy_b3[l].reshape(1, -1), deltas)
        ncb = HY_WIDTH // HY_CB
        seqs = [(feats_lat, t_lat, tabs_lat, 0, H_lat)]
        if need_ctx:
            seqs.append((feats_ctx, t_ctx, tabs_ctx, H_lat // H_ctx, H_ctx))
        ohs = []
        for feats, tpos, tabs, rb, H in seqs:
            fa, fb = _hyfilt_call(feats, tpos, *filt_args)
            kr, ki = _hyspec_call(fa, fb, tabs)
            zz = _hyconv_call(hp4, rb, 0, hp4, rb, ncb, None, 0, kr, ki, hy_bias[l], 0, tabs, H, F32)
            oh = _hyconv_call(zz, 0, 0, hp4, rb, 2 * ncb, gh4, rb, kr, ki, hy_bias[l], 1, tabs, H, BF16)
            ohs.append(oh.reshape(B, H * DFT2, HY_WIDTH))
        oh = jnp.concatenate(ohs, axis=1) if need_ctx else ohs[0]

        hcat = _outproj_call(yg, od, oh, hcat, modcat, w_out[l].astype(BF16), ssd_norm_w[l], ln_g[l], ln_b[l],
                             n_lat_tiles, n_tiles if need_ctx else n_lat_tiles, alpha)
    return hcat[:, :L]
```

```python
import functools
import math

import numpy as np
import jax
import jax.numpy as jnp
from jax import lax
from jax.experimental import pallas as pl
from jax.experimental.pallas import tpu as pltpu

F32 = jnp.float32
BF16 = jnp.bfloat16

D_MODEL = 1024
GRID_W = 64
SSD_HEADS = 16
SSD_HEAD_DIM = 64
SSD_WIDTH = SSD_HEADS * SSD_HEAD_DIM
SSD_GROUPS = 4
SSD_HPG = SSD_HEADS // SSD_GROUPS
SSD_STATE = 128
SSD_GN = SSD_GROUPS * SSD_STATE
SSD_XBC = SSD_WIDTH + 2 * SSD_GN
SSD_DT = 2 * SSD_HEADS
SSD_CHUNK = 128
GROUP_COLS = SSD_HPG * SSD_HEAD_DIM
DIFF_HEADS = 4
DIFF_DIM = 64
DIFF_WIDTH = DIFF_HEADS * 2 * DIFF_DIM
DIFF_SCALE = DIFF_DIM ** -0.5
LOG2E = math.log2(math.e)
ROPE_BASE = 10000.0
ROPE_FREQS = DIFF_DIM // 4
HY_WIDTH = 512
HY_ORDER = 2
HY_BANDS = 16
HY_EMB = 1 + 2 * HY_BANDS
HY_HIDDEN = 64
HY_MIN_DECAY = math.log(1e-2) / 1.5
HY_MAX_DECAY = math.log(1e-2) / 0.3
IN_SIZES = (SSD_WIDTH, SSD_XBC, SSD_DT, DIFF_WIDTH, DIFF_WIDTH, DIFF_WIDTH, DIFF_WIDTH, 3 * HY_WIDTH, HY_WIDTH)
IN_OFFS = tuple(int(s) for s in np.cumsum((0,) + IN_SIZES))
LN_EPS = 1e-5
RMS_EPS = 1e-6

LANES = 128
TM = 256
TQ = 512
DFT2 = 64
SUB = 8
HY_CB = 256
HY_UNROLL = 8
VMEM_BIG = 56 * 2 ** 20
NEG_BIG = -1e30

_W_Z, _W_XBC, _W_Q, _W_QP, _W_K, _W_KP, _W_V, _W_GD, _W_HP, _W_GH, _W_DT, _W_END = (
    0, 1024, 3072, 3584, 4096, 4608, 5120, 5632, 6144, 7680, 8192, 8320)


def _cparams(sem, vmem=None):
    return pltpu.CompilerParams(dimension_semantics=sem, vmem_limit_bytes=vmem)


def _silu(x):
    return x * (1.0 / (1.0 + jnp.exp(-x)))


def _dot(a, b):
    return jnp.dot(a, b, preferred_element_type=F32)


def _split3(x):
    x1 = x.astype(BF16)
    r1 = x - x1.astype(F32)
    x2 = r1.astype(BF16)
    x3 = (r1 - x2.astype(F32)).astype(BF16)
    return x1, x2, x3


def _dot3(a, b):
    a1 = a.astype(BF16)
    a2 = (a - a1.astype(F32)).astype(BF16)
    b1 = b.astype(BF16)
    b2 = (b - b1.astype(F32)).astype(BF16)
    return _dot(a1, b1) + (_dot(a1, b2) + _dot(a2, b1))


def _ada_kernel(c_ref, w_ref, b_ref, o_ref):
    o_ref[0] = _dot3(_silu(c_ref[...]), w_ref[0]) + b_ref[0]


def _ada_call(cc, w_ada, b_ada):
    depth, d, n3 = w_ada.shape
    rows = cc.shape[0]
    nb = n3 // d
    return pl.pallas_call(
        _ada_kernel,
        out_shape=jax.ShapeDtypeStruct((depth, rows, n3), F32),
        grid=(depth, nb),
        in_specs=[pl.BlockSpec((rows, d), lambda l, j: (0, 0)),
                  pl.BlockSpec((1, d, d), lambda l, j: (l, 0, j)),
                  pl.BlockSpec((1, 1, d), lambda l, j: (l, 0, j))],
        out_specs=pl.BlockSpec((1, rows, d), lambda l, j: (l, 0, j)),
        compiler_params=_cparams(("arbitrary", "arbitrary")),
        name="ada_mod",
    )(cc, w_ada, b_ada.reshape(depth, 1, n3))


def _inproj_kernel(h_ref, mod_ref, w_ref, dtb_ref, cos_ref, sin_ref,
                   z_ref, xbc_ref, q_ref, k_ref, v_ref, gd_ref, hp_ref, gh_ref, dt_ref):
    nb, tm, d = h_ref.shape
    us = []
    for j in range(nb):
        x = h_ref[j]
        xc = x - jnp.mean(x, axis=-1, keepdims=True)
        var = jnp.mean(xc * xc, axis=-1, keepdims=True)
        xn = xc * lax.rsqrt(var + LN_EPS)
        shift = mod_ref[j, 0, :, 0:d]
        scale = mod_ref[j, 0, :, d:2 * d]
        us.append((xn * (1.0 + scale) + shift).astype(BF16))
    u = jnp.concatenate(us, axis=0) if nb > 1 else us[0]

    def mm(c0, c1):
        return _dot(u, w_ref[:, c0:c1])

    def put(ref, val):
        for j in range(nb):
            ref[j] = val[j * tm:(j + 1) * tm].astype(ref.dtype)

    put(z_ref, mm(_W_Z, _W_XBC))
    put(xbc_ref, mm(_W_XBC, _W_Q))
    reps = DIFF_WIDTH // LANES
    cos = jnp.concatenate([cos_ref[...]] * reps, axis=1)
    sin = jnp.concatenate([sin_ref[...]] * reps, axis=1)
    if nb > 1:
        cos = jnp.concatenate([cos] * nb, axis=0)
        sin = jnp.concatenate([sin] * nb, axis=0)
    put(q_ref, mm(_W_Q, _W_QP) * cos + mm(_W_QP, _W_K) * sin)
    put(k_ref, mm(_W_K, _W_KP) * cos + mm(_W_KP, _W_V) * sin)
    put(v_ref, mm(_W_V, _W_GD))
    put(gd_ref, mm(_W_GD, _W_HP))
    put(hp_ref, mm(_W_HP, _W_GH))
    put(gh_ref, mm(_W_GH, _W_DT))
    dtx = mm(_W_DT, _W_END) + dtb_ref[...]
    put(dt_ref, jnp.maximum(dtx, 0.0) + jnp.log1p(jnp.exp(-jnp.abs(dtx))))


def _inproj_call(hcat, modcat, w_cat, dtb, rope_cos, rope_sin, n_lat_tiles):
    B, T, D = hcat.shape
    nb = 2 if B % 2 == 0 else 1
    nt = T // TM

    def kind(i):
        return jnp.where(i >= n_lat_tiles, 1, 0)

    widths = (SSD_WIDTH, SSD_XBC, DIFF_WIDTH, DIFF_WIDTH, DIFF_WIDTH, DIFF_WIDTH, 3 * HY_WIDTH, HY_WIDTH, LANES)
    dtypes = (BF16,) * 8 + (F32,)
    out_shape = tuple(jax.ShapeDtypeStruct((B, T, w), dt) for w, dt in zip(widths, dtypes))
    out_specs = tuple(pl.BlockSpec((nb, TM, w), lambda b, i: (b, i, 0)) for w in widths)
    return pl.pallas_call(
        _inproj_kernel,
        out_shape=out_shape,
        grid=(B // nb, nt),
        in_specs=[pl.BlockSpec((nb, TM, D), lambda b, i: (b, i, 0)),
                  pl.BlockSpec((nb, 1, 1, 3 * D), lambda b, i: (b, kind(i), 0, 0)),
                  pl.BlockSpec((D, _W_END), lambda b, i: (0, 0), pipeline_mode=pl.Buffered(1)),
                  pl.BlockSpec((1, LANES), lambda b, i: (0, 0)),
                  pl.BlockSpec((TM, LANES), lambda b, i: (i, 0)),
                  pl.BlockSpec((TM, LANES), lambda b, i: (i, 0))],
        out_specs=out_specs,
        compiler_params=_cparams(("arbitrary", "arbitrary"), VMEM_BIG),
        name="ln_inproj",
    )(hcat, modcat, w_cat, dtb, rope_cos, rope_sin)


def _shortconv_kernel(x_ref, w_ref, b_ref, o_ref, *, n_lat_chunks, n_chunks, act):
    q = SSD_CHUNK
    pack = 16
    w0 = w_ref[0:1, :]
    w1 = w_ref[1:2, :]
    w2 = w_ref[2:3, :]
    bias = b_ref[...]
    rowid = lax.broadcasted_iota(jnp.int32, (q, 1), 0)

    def body(c, carry):
        r0 = pl.multiple_of(c * q, q)
        cur = x_ref[0, pl.ds(r0, q), :].astype(F32)
        p0 = pl.multiple_of(jnp.maximum(r0 - pack, 0), pack)
        n0 = pl.multiple_of(jnp.minimum(r0 + q, (n_chunks - 1) * q), pack)
        prow = x_ref[0, pl.ds(p0, pack), :].astype(F32)[pack - 1:pack, :]
        nrow = x_ref[0, pl.ds(n0, pack), :].astype(F32)[0:1, :]
        pvalid = jnp.logical_and(c != 0, c != n_lat_chunks)
        nvalid = jnp.logical_and(c != n_lat_chunks - 1, c != n_chunks - 1)
        prow = jnp.where(pvalid, prow, 0.0)
        nrow = jnp.where(nvalid, nrow, 0.0)
        prev = jnp.where(rowid == 0, prow, pltpu.roll(cur, 1, 0))
        nxt = jnp.where(rowid == q - 1, nrow, pltpu.roll(cur, q - 1, 0))
        y = prev * w0 + cur * w1 + nxt * w2 + bias
        if act:
            y = _silu(y)
        o_ref[0, pl.ds(r0, q), :] = y.astype(o_ref.dtype)
        return carry

    lax.fori_loop(0, n_chunks, body, 0)


def _shortconv_call(x, w, b, n_lat_chunks, act, out_dtype):
    B, T, C = x.shape
    cb = 256
    kern = functools.partial(_shortconv_kernel, n_lat_chunks=n_lat_chunks, n_chunks=T // SSD_CHUNK, act=act)
    return pl.pallas_call(
        kern,
        out_shape=jax.ShapeDtypeStruct((B, T, C), out_dtype),
        grid=(B, C // cb),
        in_specs=[pl.BlockSpec((1, T, cb), lambda b, j: (b, 0, j)),
                  pl.BlockSpec((3, cb), lambda b, j: (0, j)),
                  pl.BlockSpec((1, cb), lambda b, j: (0, j))],
        out_specs=pl.BlockSpec((1, T, cb), lambda b, j: (b, 0, j)),
        compiler_params=_cparams(("arbitrary", "arbitrary")),
        name="shortconv_act" if act else "shortconv",
    )(x, w, b.reshape(1, C))


def _ssd_kernel(x_ref, b_ref, c_ref, z_ref, dtc_ref, dtr_ref, alr_ref, alc_ref, dsk_ref, o_ref,
                yacc, hst, *, n_lat_chunks, n_ctx_chunks):
    q = SSD_CHUNK
    r_heads = SSD_HPG
    gc = GROUP_COLS
    nch = n_lat_chunks + n_ctx_chunks
    li = lax.broadcasted_iota(jnp.int32, (q, q), 0)
    si = lax.broadcasted_iota(jnp.int32, (q, q), 1)
    low = si <= li
    upp = si >= li
    tri_l = jnp.where(low, 1.0, 0.0).astype(BF16)
    tri_u = jnp.where(upp, 1.0, 0.0).astype(BF16)
    a_row = -jnp.exp(alr_ref[0])
    a_col = -jnp.exp(alc_ref[0])
    dsk = dsk_ref[0]
    lane_head = lax.broadcasted_iota(jnp.int32, (1, gc), 1) // SSD_HEAD_DIM

    def cum_left(tri, x):
        x1, x2, x3 = _split3(x)
        return _dot(tri, x1) + (_dot(tri, x2) + _dot(tri, x3))

    def cum_right(x, tri):
        x1, x2, x3 = _split3(x)
        return _dot(x1, tri) + (_dot(x2, tri) + _dot(x3, tri))

    def expand(cols, c0):
        out = cols[:, c0 + r_heads - 1:c0 + r_heads]
        for r in range(r_heads - 2, -1, -1):
            out = jnp.where(lane_head == r, cols[:, c0 + r:c0 + r + 1], out)
        return out

    def load(c):
        r0 = pl.multiple_of(c * q, q)
        xc = x_ref[0, pl.ds(r0, q), :].astype(F32)
        bc = b_ref[0, pl.ds(r0, q), :]
        cc = c_ref[0, pl.ds(r0, q), :]
        dtc = dtc_ref[0, 0, pl.ds(r0, q), :]
        return r0, xc, bc, cc, dtc

    def fwd_body(j, carry):
        c = jnp.where(j < n_ctx_chunks, n_lat_chunks + j, j - n_ctx_chunks)
        r0, xc, bc, cc, dtc = load(c)
        dtr = dtr_ref[0, 0, c]
        dta_c = dtc * a_row
        dta_r = dtr * a_col
        la_cf = cum_left(tri_l, dta_c)
        la_cb = cum_left(tri_u, dta_c)
        la_rf = cum_right(dta_r, tri_u)
        la_rb = cum_right(dta_r, tri_l)
        cb = lax.dot_general(cc, bc, (((1,), (1,)), ((), ())), preferred_element_type=F32)
        xf = xc * expand(dtc, 0)
        xb = xc * expand(dtc, r_heads)
        y = dsk * xc
        for r in range(r_heads):
            seg_f = la_cf[:, r:r + 1] - la_rf[r:r + 1, :]
            seg_b = la_cb[:, r_heads + r:r_heads + r + 1] - la_rb[r_heads + r:r_heads + r + 1, :]
            wf = (jnp.exp(jnp.where(low, seg_f, NEG_BIG)) * cb).astype(BF16)
            wb = (jnp.exp(jnp.where(upp, seg_b, NEG_BIG)) * cb).astype(BF16)
            sel = lane_head == r
            y = y + _dot(wf, jnp.where(sel, xf, 0.0).astype(BF16))
            y = y + _dot(wb, jnp.where(sel, xb, 0.0).astype(BF16))
        h = hst[...]
        y = y + _dot(cc, h.astype(BF16)) * expand(jnp.exp(la_cf), 0)
        yacc[pl.ds(r0, q), :] = y
        last = la_cf[q - 1:q, :]
        xs = (xf * expand(jnp.exp(last - la_cf), 0)).astype(BF16)
        s_new = lax.dot_general(bc, xs, (((0,), (0,)), ((), ())), preferred_element_type=F32)
        hst[...] = h * expand(jnp.exp(last), 0) + s_new
        return carry

    hst[...] = jnp.zeros_like(hst)
    lax.fori_loop(0, nch, fwd_body, 0)

    def bwd_body(j, carry):
        c = nch - 1 - j
        r0, xc, bc, cc, dtc = load(c)
        la_cb = cum_left(tri_u, dtc * a_row)
        h = hst[...]
        y = yacc[pl.ds(r0, q), :] + _dot(cc, h.astype(BF16)) * expand(jnp.exp(la_cb), r_heads)
        first = la_cb[0:1, :]
        xs = (xc * expand(dtc * jnp.exp(first - la_cb), r_heads)).astype(BF16)
        s_new = lax.dot_general(bc, xs, (((0,), (0,)), ((), ())), preferred_element_type=F32)
        hst[...] = h * expand(jnp.exp(first), r_heads) + s_new
        zz = z_ref[0, pl.ds(r0, q), :].astype(F32)
        o_ref[0, pl.ds(r0, q), :] = (y * _silu(zz)).astype(o_ref.dtype)
        return carry

    hst[...] = jnp.zeros_like(hst)
    lax.fori_loop(0, nch, bwd_body, 0)


def _ssd_call(xbc_c, z, dt_col, dt_row, alog_row, alog_col, dsk, n_lat_chunks, n_ctx_chunks):
    B, T, _ = xbc_c.shape
    G = SSD_GROUPS
    gc = GROUP_COLS
    nch = T // SSD_CHUNK
    b_off = SSD_WIDTH // SSD_STATE
    c_off = (SSD_WIDTH + SSD_GN) // SSD_STATE
    kern = functools.partial(_ssd_kernel, n_lat_chunks=n_lat_chunks, n_ctx_chunks=n_ctx_chunks)
    return pl.pallas_call(
        kern,
        out_shape=jax.ShapeDtypeStruct((B, T, SSD_WIDTH), BF16),
        grid=(B, G),
        in_specs=[pl.BlockSpec((1, T, gc), lambda b, g: (b, 0, g)),
                  pl.BlockSpec((1, T, SSD_STATE), lambda b, g: (b, 0, b_off + g)),
                  pl.BlockSpec((1, T, SSD_STATE), lambda b, g: (b, 0, c_off + g)),
                  pl.BlockSpec((1, T, gc), lambda b, g: (b, 0, g)),
                  pl.BlockSpec((1, 1, T, 2 * SSD_HPG), lambda b, g: (b, g, 0, 0)),
                  pl.BlockSpec((1, 1, nch, 2 * SSD_HPG, SSD_CHUNK), lambda b, g: (b, g, 0, 0, 0)),
                  pl.BlockSpec((1, 1, 2 * SSD_HPG), lambda b, g: (g, 0, 0)),
                  pl.BlockSpec((1, 2 * SSD_HPG, 1), lambda b, g: (g, 0, 0)),
                  pl.BlockSpec((1, 1, gc), lambda b, g: (g, 0, 0))],
        out_specs=pl.BlockSpec((1, T, gc), lambda b, g: (b, 0, g)),
        scratch_shapes=[pltpu.VMEM((T, gc), F32), pltpu.VMEM((SSD_STATE, gc), F32)],
        compiler_params=_cparams(("arbitrary", "arbitrary"), VMEM_BIG),
        name="ssd",
    )(xbc_c, xbc_c, xbc_c, z, dt_col, dt_row, alog_row, alog_col, dsk)


def _attn_kernel(lam_ref, q_ref, k_ref, v_ref, gd_ref, o_ref, vext, *, lam_init):
    hw = 2 * DIFF_DIM

    @pl.when(pl.program_id(2) == 0)
    def _():
        col = lax.broadcasted_iota(jnp.int32, (v_ref.shape[1], hw), 1)
        vext[:, :hw] = v_ref[0]
        vext[:, hw:] = jnp.where(col == 0, 1.0, 0.0).astype(BF16)

    lane = lax.broadcasted_iota(jnp.int32, (1, hw), 1)
    qt = q_ref[0]
    kt = k_ref[0]
    zero = jnp.zeros_like(qt)
    heads = []
    for m in range(2):
        qm = jnp.where((lane >= DIFF_DIM) == (m == 1), qt, zero)
        s = lax.dot_general(qm, kt, (((1,), (1,)), ((), ())), preferred_element_type=F32)
        p = jnp.exp2(s - jnp.max(s, axis=1, keepdims=True)).astype(BF16)
        ov = _dot(p, vext[...])
        heads.append(ov[:, :hw] / ov[:, hw:hw + 1])

    lv = lam_ref[...]
    lam = (jnp.exp(jnp.sum(lv[0:1] * lv[1:2], axis=1, keepdims=True))
           - jnp.exp(jnp.sum(lv[2:3] * lv[3:4], axis=1, keepdims=True)) + lam_init)
    o = heads[0] - lam * heads[1]
    o = o * lax.rsqrt(jnp.mean(o * o, axis=1, keepdims=True) + RMS_EPS) * (1.0 - lam_init)
    o_ref[0] = (o * _silu(gd_ref[0].astype(F32))).astype(o_ref.dtype)


def _attn_call(lam_vecs, q, k, v, gd, tq, q_tile0, n_q_tiles, kv_rows, kv_blk, lam_init):
    B = q.shape[0]
    hw = 2 * DIFF_DIM
    kern = functools.partial(_attn_kernel, lam_init=lam_init)
    return pl.pallas_call(
        kern,
        out_shape=jax.ShapeDtypeStruct((B, n_q_tiles * tq, DIFF_WIDTH), BF16),
        grid=(B, DIFF_HEADS, n_q_tiles),
        in_specs=[pl.BlockSpec((4, DIFF_DIM), lambda b, h, i: (0, 0)),
                  pl.BlockSpec((1, tq, hw), lambda b, h, i: (b, q_tile0 + i, h)),
                  pl.BlockSpec((1, kv_rows, hw), lambda b, h, i: (b, kv_blk, h)),
                  pl.BlockSpec((1, kv_rows, hw), lambda b, h, i: (b, kv_blk, h)),
                  pl.BlockSpec((1, tq, hw), lambda b, h, i: (b, q_tile0 + i, h))],
        out_specs=pl.BlockSpec((1, tq, hw), lambda b, h, i: (b, i, h)),
        scratch_shapes=[pltpu.VMEM((kv_rows, 2 * hw), BF16)],
        compiler_params=_cparams(("arbitrary", "arbitrary", "arbitrary"), VMEM_BIG),
        name="diff_attn",
    )(lam_vecs, q, k, v, gd)


def _hyfilt_kernel(feat_ref, t_ref, w1_ref, b1_ref, w2_ref, b2_ref, fr_ref, w3a_ref, b3a_ref,
                   w3b_ref, b3b_ref, dl_ref, oa_ref, ob_ref):
    fr = fr_ref[...]
    h1 = jnp.sin(fr * (_dot3(feat_ref[...], w1_ref[...]) + b1_ref[...]))
    h2 = jnp.sin(fr * (_dot3(h1, w2_ref[...]) + b2_ref[...]))
    win = jnp.exp(-t_ref[...] * dl_ref[...])
    ca = (_dot3(h2, w3a_ref[...]) + b3a_ref[...]) * win
    cb = (_dot3(h2, w3b_ref[...]) + b3b_ref[...]) * win
    rowid = lax.broadcasted_iota(jnp.int32, (ca.shape[0], 1), 0)
    cb = jnp.where(rowid == 0, 0.0, cb)
    energy = jnp.sum(ca * ca, axis=0, keepdims=True) + jnp.sum(cb * cb, axis=0, keepdims=True)
    sc = lax.rsqrt(energy + RMS_EPS)
    oa_ref[0] = ca * sc
    ob_ref[0] = cb * sc


def _hyfilt_call(feats, tpos, w1p, b1, w2, b2, freq, w3, b3, deltas):
    L = feats.shape[0]
    cb = HY_CB
    ncb = HY_WIDTH // cb
    hid = HY_HIDDEN

    def full(shape):
        return pl.BlockSpec(shape, lambda o, j: (0,) * len(shape))

    def w3spec(d):
        return pl.BlockSpec((hid, cb), lambda o, j: (0, (d * HY_ORDER + o) * ncb + j))

    def b3spec(d):
        return pl.BlockSpec((1, cb), lambda o, j: (0, (d * HY_ORDER + o) * ncb + j))

    out = jax.ShapeDtypeStruct((HY_ORDER, L, HY_WIDTH), F32)
    ospec = pl.BlockSpec((1, L, cb), lambda o, j: (o, 0, j))
    return pl.pallas_call(
        _hyfilt_kernel,
        out_shape=(out, out),
        grid=(HY_ORDER, ncb),
        in_specs=[full((L, hid)), full((L, 1)), full((hid, hid)), full((1, hid)), full((hid, hid)),
                  full((1, hid)), full((1, hid)), w3spec(0), b3spec(0), w3spec(1), b3spec(1),
                  pl.BlockSpec((1, cb), lambda o, j: (0, j))],
        out_specs=(ospec, ospec),
        compiler_params=_cparams(("arbitrary", "arbitrary"), VMEM_BIG),
        name="hy_filter",
    )(feats, tpos, w1p, b1, w2, b2, freq, w3, b3, w3, b3, deltas)


def _twiddle(twc_ref, tws_ref, k1, width):
    reps = width // LANES
    c = jnp.concatenate([twc_ref[k1]] * reps, axis=1)
    s = jnp.concatenate([tws_ref[k1]] * reps, axis=1)
    return c, s


def _hyspec_kernel(fa_ref, fb_ref, gl1_ref, twc_ref, tws_ref, f64_ref, kr_ref, ki_ref, wsr, wsi, *, k1c):
    kc = pl.program_id(2)
    cw = fa_ref.shape[-1]
    gl1 = gl1_ref[0]
    for j in range(DFT2 // SUB):
        xa = fa_ref[0, :, SUB * j:SUB * (j + 1), :]
        xb = fb_ref[0, :, SUB * j:SUB * (j + 1), :]
        h = xa.shape[0]
        xs = jnp.concatenate([xa.reshape(h * SUB, cw), xb.reshape(h * SUB, cw)], axis=1).astype(BF16)
        a = _dot(gl1, xs)
        wsr[:, SUB * j:SUB * (j + 1), :] = a[:k1c * SUB].reshape(k1c, SUB, 2 * cw)
        wsi[:, SUB * j:SUB * (j + 1), :] = a[k1c * SUB:].reshape(k1c, SUB, 2 * cw)
    f64 = f64_ref[...]

    def body(k, carry):
        c, s = _twiddle(twc_ref, tws_ref, kc * k1c + k, 2 * cw)
        ar = wsr[k]
        ai = wsi[k]
        br = ar * c + ai * s
        bi = ai * c - ar * s
        x = _dot(f64, jnp.concatenate([br, bi], axis=0).astype(BF16))
        xr = x[:DFT2]
        xi = x[DFT2:]
        r0 = pl.multiple_of(k * DFT2, DFT2)
        kr_ref[0, pl.ds(r0, DFT2), :] = (xr[:, :cw] + xr[:, cw:]).astype(kr_ref.dtype)
        ki_ref[0, pl.ds(r0, DFT2), :] = (xi[:, :cw] - xi[:, cw:]).astype(ki_ref.dtype)
        return carry

    lax.fori_loop(0, k1c, body, 0, unroll=min(HY_UNROLL, k1c))


def _hyspec_call(fa, fb, tabs):
    _, L, _ = fa.shape
    H = L // DFT2
    n1 = 2 * H
    k1c = tabs["k1c"]
    nkc = n1 // k1c
    cb = HY_CB
    ncb = HY_WIDTH // cb
    fa4 = fa.reshape(HY_ORDER, H, DFT2, HY_WIDTH)
    fb4 = fb.reshape(HY_ORDER, H, DFT2, HY_WIDTH)
    out = jax.ShapeDtypeStruct((HY_ORDER, n1 * DFT2, HY_WIDTH), BF16)
    ospec = pl.BlockSpec((1, k1c * DFT2, cb), lambda o, j, kc: (o, kc, j))
    fspec = pl.BlockSpec((1, H, DFT2, cb), lambda o, j, kc: (o, 0, 0, j))
    kern = functools.partial(_hyspec_kernel, k1c=k1c)
    return pl.pallas_call(
        kern,
        out_shape=(out, out),
        grid=(HY_ORDER, ncb, nkc),
        in_specs=[fspec, fspec,
                  pl.BlockSpec((1, 2 * k1c * SUB, H * SUB), lambda o, j, kc: (kc, 0, 0)),
                  pl.BlockSpec((n1, DFT2, LANES), lambda o, j, kc: (0, 0, 0), pipeline_mode=pl.Buffered(1)),
                  pl.BlockSpec((n1, DFT2, LANES), lambda o, j, kc: (0, 0, 0), pipeline_mode=pl.Buffered(1)),
                  pl.BlockSpec((2 * DFT2, 2 * DFT2), lambda o, j, kc: (0, 0))],
        out_specs=(ospec, ospec),
        scratch_shapes=[pltpu.VMEM((k1c, DFT2, 2 * cb), F32), pltpu.VMEM((k1c, DFT2, 2 * cb), F32)],
        compiler_params=_cparams(("arbitrary", "arbitrary", "arbitrary"), VMEM_BIG),
        name="hy_spectrum",
    )(fa4, fb4, tabs["gl1"], tabs["twc"], tabs["tws"], tabs["f64"])


def _hyconv_kernel(*refs, k1c, nkc, n_fft, final):
    u_ref, g_ref = refs[0], refs[1]
    pos = 2
    gh_ref = None
    if final:
        gh_ref = refs[pos]
        pos += 1
    kr_ref, ki_ref, bias_ref, gl1_ref, gl1i_ref, twc_ref, tws_ref, f64_ref, f64i_ref = refs[pos:pos + 9]
    o_ref, wsr, wsi, acc = refs[pos + 9:pos + 13]
    kc = pl.program_id(2)
    cw = u_ref.shape[-1]
    h = u_ref.shape[1]

    gl1 = gl1_ref[0]
    for j in range(DFT2 // SUB):
        xs = u_ref[0, :, SUB * j:SUB * (j + 1), :].reshape(h * SUB, cw).astype(BF16)
        a = _dot(gl1, xs)
        wsr[:, SUB * j:SUB * (j + 1), :] = a[:k1c * SUB].reshape(k1c, SUB, cw)
        wsi[:, SUB * j:SUB * (j + 1), :] = a[k1c * SUB:].reshape(k1c, SUB, cw)
    f64 = f64_ref[...]
    f64i = f64i_ref[...]

    def body(k, carry):
        c, s = _twiddle(twc_ref, tws_ref, kc * k1c + k, cw)
        ar = wsr[k]
        ai = wsi[k]
        br = ar * c + ai * s
        bi = ai * c - ar * s
        x = _dot(f64, jnp.concatenate([br, bi], axis=0).astype(BF16))
        xr = x[:DFT2]
        xi = x[DFT2:]
        r0 = pl.multiple_of(k * DFT2, DFT2)
        kr = kr_ref[0, pl.ds(r0, DFT2), :].astype(F32)
        ki = ki_ref[0, pl.ds(r0, DFT2), :].astype(F32)
        yr = xr * kr - xi * ki
        yi = xr * ki + xi * kr
        cc = _dot(f64i, jnp.concatenate([yr, yi], axis=0).astype(BF16))
        cr = cc[:DFT2]
        ci = cc[DFT2:]
        wsr[k] = cr * c - ci * s
        wsi[k] = ci * c + cr * s
        return carry

    lax.fori_loop(0, k1c, body, 0, unroll=min(HY_UNROLL, k1c))

    @pl.when(kc == 0)
    def _():
        acc[...] = jnp.zeros_like(acc)

    gl1i = gl1i_ref[0]
    for j in range(DFT2 // SUB):
        dr = wsr[:, SUB * j:SUB * (j + 1), :].reshape(k1c * SUB, cw)
        di = wsi[:, SUB * j:SUB * (j + 1), :].reshape(k1c * SUB, cw)
        yj = _dot(gl1i, jnp.concatenate([dr, di], axis=0).astype(BF16))
        acc[:, SUB * j:SUB * (j + 1), :] += yj.reshape(h, SUB, cw)

    @pl.when(kc == nkc - 1)
    def _():
        u = u_ref[0]
        y = g_ref[0] * (acc[...] * (1.0 / n_fft) + bias_ref[0] * u)
        if final:
            y = y * _silu(gh_ref[0].astype(F32))
        o_ref[0] = y.astype(o_ref.dtype)


def _hyconv_call(u4, u_rb, u_cb, g4, g_rb, g_cb, gh4, gh_rb, kr, ki, bias, order, tabs, H, out_dtype):
    B = u4.shape[0]
    k1c = tabs["k1c"]
    n1 = 2 * H
    nkc = n1 // k1c
    cb = HY_CB
    final = gh4 is not None

    def seq_spec(rb, off):
        return pl.BlockSpec((1, H, DFT2, cb), lambda b, j, kc: (b, rb, 0, off + j))

    in_specs = [seq_spec(u_rb, u_cb), seq_spec(g_rb, g_cb)]
    args = [u4, g4]
    if final:
        in_specs.append(seq_spec(gh_rb, 0))
        args.append(gh4)
    in_specs += [
        pl.BlockSpec((1, k1c * DFT2, cb), lambda b, j, kc: (order, kc, j)),
        pl.BlockSpec((1, k1c * DFT2, cb), lambda b, j, kc: (order, kc, j)),
        pl.BlockSpec((1, 1, cb), lambda b, j, kc: (order, 0, j)),
        pl.BlockSpec((1, 2 * k1c * SUB, H * SUB), lambda b, j, kc: (kc, 0, 0)),
        pl.BlockSpec((1, H * SUB, 2 * k1c * SUB), lambda b, j, kc: (kc, 0, 0)),
        pl.BlockSpec((n1, DFT2, LANES), lambda b, j, kc: (0, 0, 0), pipeline_mode=pl.Buffered(1)),
        pl.BlockSpec((n1, DFT2, LANES), lambda b, j, kc: (0, 0, 0), pipeline_mode=pl.Buffered(1)),
        pl.BlockSpec((2 * DFT2, 2 * DFT2), lambda b, j, kc: (0, 0)),
        pl.BlockSpec((2 * DFT2, 2 * DFT2), lambda b, j, kc: (0, 0)),
    ]
    args += [kr, ki, bias.reshape(HY_ORDER, 1, HY_WIDTH), tabs["gl1"], tabs["gl1i"], tabs["twc"], tabs["tws"],
             tabs["f64"], tabs["f64i"]]
    kern = functools.partial(_hyconv_kernel, k1c=k1c, nkc=nkc, n_fft=float(n1 * DFT2), final=final)
    return pl.pallas_call(
        kern,
        out_shape=jax.ShapeDtypeStruct((B, H, DFT2, HY_WIDTH), out_dtype),
        grid=(B, HY_WIDTH // cb, nkc),
        in_specs=in_specs,
        out_specs=pl.BlockSpec((1, H, DFT2, cb), lambda b, j, kc: (b, 0, 0, j)),
        scratch_shapes=[pltpu.VMEM((k1c, DFT2, cb), F32), pltpu.VMEM((k1c, DFT2, cb), F32),
                        pltpu.VMEM((H, DFT2, cb), F32)],
        compiler_params=_cparams(("arbitrary", "arbitrary", "arbitrary"), VMEM_BIG),
        name="hy_conv",
    )(*args)


def _dft_tables(H):
    n1 = 2 * H
    n = n1 * DFT2
    k1c = min(n1, 32)
    nkc = n1 // k1c
    k1 = np.arange(n1, dtype=np.float64)
    th = 2.0 * np.pi * np.outer(k1, np.arange(H, dtype=np.float64)) / n1
    eye = np.eye(SUB)
    gre = np.kron(np.cos(th), eye).reshape(nkc, k1c * SUB, H * SUB)
    gim = np.kron(-np.sin(th), eye).reshape(nkc, k1c * SUB, H * SUB)
    gl1 = np.concatenate([gre, gim], axis=1)
    gire = np.kron(np.cos(th).T, eye).reshape(H * SUB, nkc, k1c * SUB).transpose(1, 0, 2)
    giim = np.kron(-np.sin(th).T, eye).reshape(H * SUB, nkc, k1c * SUB).transpose(1, 0, 2)
    gl1i = np.concatenate([gire, giim], axis=2)
    ph = 2.0 * np.pi * np.outer(k1, np.arange(DFT2, dtype=np.float64)) / n
    twc = np.repeat(np.cos(ph)[:, :, None], LANES, axis=2)
    tws = np.repeat(np.sin(ph)[:, :, None], LANES, axis=2)
    ps = 2.0 * np.pi * np.outer(np.arange(DFT2), np.arange(DFT2)) / DFT2
    c64, s64 = np.cos(ps), np.sin(ps)
    f64 = np.block([[c64, s64], [-s64, c64]])
    f64i = np.block([[c64, -s64], [s64, c64]])
    as_bf = lambda a: jnp.asarray(a, F32).astype(BF16)
    return dict(k1c=k1c, gl1=as_bf(gl1), gl1i=as_bf(gl1i), twc=jnp.asarray(twc, F32),
                tws=jnp.asarray(tws, F32), f64=as_bf(f64), f64i=as_bf(f64i))


def _hy_features(L):
    t = np.linspace(0.0, 1.0, L)[:, None]
    wpos = (2.0 * math.pi / L) * np.arange(L, dtype=np.float64)[:, None]
    bands = np.linspace(1e-4, HY_BANDS - 1, HY_BANDS)
    feats = np.concatenate([t, np.cos(wpos * bands), -np.sin(wpos * bands)], -1)
    feats = np.pad(feats, ((0, 0), (0, HY_HIDDEN - HY_EMB)))
    return jnp.asarray(feats, F32), jnp.asarray(t, F32)


def _outproj_kernel(y_ref, od_ref, oh_ref, h_ref, mod_ref, w_ref, nw_ref, g_ref, b_ref, o_ref, *, alpha):
    nb, tm, d = h_ref.shape
    sw = y_ref.shape[-1]
    dw = od_ref.shape[-1]
    ys, ods, ohs = [], [], []
    for j in range(nb):
        y = y_ref[j].astype(F32)
        y = y * lax.rsqrt(jnp.mean(y * y, axis=-1, keepdims=True) + RMS_EPS) * nw_ref[...]
        ys.append(y.astype(BF16))
        ods.append(od_ref[j])
        ohs.append(oh_ref[j])
    cat = (lambda xs: jnp.concatenate(xs, axis=0)) if nb > 1 else (lambda xs: xs[0])
    o = (_dot(cat(ys), w_ref[0:sw, :]) + _dot(cat(ods), w_ref[sw:sw + dw, :])
         + _dot(cat(ohs), w_ref[sw + dw:, :]))
    for j in range(nb):
        gate = mod_ref[j, 0, :, 2 * d:3 * d]
        r = alpha * h_ref[j] + gate * o[j * tm:(j + 1) * tm]
        rc = r - jnp.mean(r, axis=-1, keepdims=True)
        var = jnp.mean(rc * rc, axis=-1, keepdims=True)
        o_ref[j] = rc * lax.rsqrt(var + LN_EPS) * g_ref[...] + b_ref[...]


def _outproj_call(yg, od, oh, hcat, modcat, w_out, norm_w, ln_g, ln_b, n_lat_tiles, n_tiles, alpha):
    B, T, D = hcat.shape
    nb = 2 if B % 2 == 0 else 1

    def kind(i):
        return jnp.where(i >= n_lat_tiles, 1, 0)

    def tok(w):
        return pl.BlockSpec((nb, TM, w), lambda b, i: (b, i, 0))

    def vec(w):
        return pl.BlockSpec((1, w), lambda b, i: (0, 0))

    kern = functools.partial(_outproj_kernel, alpha=alpha)
    return pl.pallas_call(
        kern,
        out_shape=jax.ShapeDtypeStruct((B, n_tiles * TM, D), F32),
        grid=(B // nb, n_tiles),
        in_specs=[tok(SSD_WIDTH), tok(DIFF_WIDTH), tok(HY_WIDTH), tok(D),
                  pl.BlockSpec((nb, 1, 1, 3 * D), lambda b, i: (b, kind(i), 0, 0)),
                  pl.BlockSpec(w_out.shape, lambda b, i: (0, 0)),
                  vec(SSD_WIDTH), vec(D), vec(D)],
        out_specs=tok(D),
        compiler_params=_cparams(("arbitrary", "arbitrary"), VMEM_BIG),
        name="outproj",
    )(yg, od, oh, hcat, modcat, w_out, norm_w.reshape(1, -1), ln_g.reshape(1, -1), ln_b.reshape(1, -1))


def _rope_tables(L, Lc):
    rows = np.repeat(np.arange(L // GRID_W), GRID_W).astype(np.float64)
    cols = np.tile(np.arange(GRID_W), L // GRID_W).astype(np.float64)
    inv = ROPE_BASE ** (-np.arange(ROPE_FREQS, dtype=np.float64) / ROPE_FREQS)
    lane = np.arange(LANES)
    f = lane % ROPE_FREQS
    is_row = (lane % DIFF_DIM) < (DIFF_DIM // 2)
    ang = np.where(is_row[None, :], rows[:, None] * inv[f][None, :], cols[:, None] * inv[f][None, :])
    cos = np.concatenate([np.cos(ang), np.ones((Lc, LANES))], 0)
    sin = np.concatenate([np.sin(ang), np.zeros((Lc, LANES))], 0)
    return jnp.asarray(cos, F32), jnp.asarray(sin, F32)


def _rot_partner_cols(w):
    n = w.shape[1]
    col = np.arange(n)
    first = (col % (2 * ROPE_FREQS)) < ROPE_FREQS
    src = np.where(first, col + ROPE_FREQS, col - ROPE_FREQS)
    sign = np.where(first, -1.0, 1.0).astype(np.float32)
    return w[:, src] * sign


def _dt_perm():
    g, d, r = np.meshgrid(np.arange(SSD_GROUPS), np.arange(2), np.arange(SSD_HPG), indexing="ij")
    return (d * SSD_HEADS + g * SSD_HPG + r).reshape(-1)


def _layout_w_in(w):
    o = IN_OFFS
    z, xbc, dt = w[:, o[0]:o[1]], w[:, o[1]:o[2]], w[:, o[2]:o[3]]
    q, k, v, gd = w[:, o[3]:o[4]] * (DIFF_SCALE * LOG2E), w[:, o[4]:o[5]], w[:, o[5]:o[6]], w[:, o[6]:o[7]]
    hp, gh = w[:, o[7]:o[8]], w[:, o[8]:o[9]]
    dtp = jnp.pad(dt[:, _dt_perm()], ((0, 0), (0, LANES - SSD_DT)))
    cat = jnp.concatenate([z, xbc, q, _rot_partner_cols(q), k, _rot_partner_cols(k), v, gd, hp, gh, dtp], axis=1)
    return cat.astype(BF16)


def kernel(x, c, ctx, c_ctx, w_ada, b_ada, w_in, ssd_conv_w, ssd_conv_b, ssd_dt_bias, ssd_a_log, ssd_d,
           ssd_norm_w, diff_lambda, hy_conv_w, hy_conv_b, hy_w1, hy_b1, hy_w2, hy_b2, hy_w3, hy_b3,
           hy_freq, hy_bias, w_out, ln_g, ln_b):
    B, L, D = x.shape
    Lc = ctx.shape[1]
    depth = w_in.shape[0]
    T = L + Lc
    assert L % TM == 0 and Lc % TM == 0 and L % GRID_W == 0 and D == D_MODEL
    n_lat_tiles, n_tiles = L // TM, T // TM
    n_lat_chunks, n_ctx_chunks = L // SSD_CHUNK, Lc // SSD_CHUNK
    nch = n_lat_chunks + n_ctx_chunks
    G, R8 = SSD_GROUPS, 2 * SSD_HPG
    alpha = (2 * depth) ** 0.25
    H_lat, H_ctx = L // DFT2, Lc // DFT2
    assert H_lat % H_ctx == 0 and L % Lc == 0

    rope_cos, rope_sin = _rope_tables(L, Lc)
    tabs_lat, tabs_ctx = _dft_tables(H_lat), _dft_tables(H_ctx)
    feats_lat, t_lat = _hy_features(L)
    feats_ctx, t_ctx = _hy_features(Lc)
    deltas = jnp.asarray(np.abs(np.linspace(HY_MIN_DECAY, HY_MAX_DECAY, HY_WIDTH))[None, :], F32)

    rows = -(-(B + 1) // SUB) * SUB
    cc = jnp.zeros((rows, D), F32).at[:B].set(c).at[B].set(c_ctx)
    mod_all = _ada_call(cc, w_ada, b_ada)

    hcat = jnp.concatenate([x, ctx], axis=1)
    perm = _dt_perm()
    for l in range(depth):
        need_ctx = l < depth - 1
        mod = mod_all[l]
        modcat = jnp.stack([mod[:B], jnp.broadcast_to(mod[B], (B, 3 * D))], axis=1)[:, :, None, :]
        dtb = jnp.pad(ssd_dt_bias[l].reshape(-1)[perm], (0, LANES - SSD_DT)).reshape(1, LANES)
        z, xbc, q, k, v, gd, hp, gh, dt = _inproj_call(
            hcat, modcat, _layout_w_in(w_in[l]), dtb, rope_cos, rope_sin, n_lat_tiles)

        xbc_c = _shortconv_call(xbc, ssd_conv_w[l], ssd_conv_b[l], n_lat_chunks, True, BF16)
        dt_g = dt[:, :, :SSD_DT].reshape(B, T, G, R8).transpose(0, 2, 1, 3)
        dt_row = dt_g.reshape(B, G, nch, SSD_CHUNK, R8).transpose(0, 1, 2, 4, 3)
        alog = ssd_a_log[l].reshape(2, G, SSD_HPG).transpose(1, 0, 2).reshape(G, R8)
        dsk = jnp.repeat(ssd_d[l].reshape(G, SSD_HPG), SSD_HEAD_DIM, axis=1).reshape(G, 1, GROUP_COLS)
        yg = _ssd_call(xbc_c, z, dt_g, dt_row, alog.reshape(G, 1, R8), alog.reshape(G, R8, 1), dsk,
                       n_lat_chunks, n_ctx_chunks)

        lam_init = 0.8 - 0.6 * math.exp(-0.3 * l)
        tq = math.gcd(TQ, L)
        od = _attn_call(diff_lambda[l], q, k, v, gd, tq, 0, L // tq, T, 0, lam_init)
        if need_ctx:
            tqc = math.gcd(tq, Lc)
            od_c = _attn_call(diff_lambda[l], q, k, v, gd, tqc, L // tqc, Lc // tqc, Lc, L // Lc, lam_init)
            od = jnp.concatenate([od, od_c], axis=1)

        hpc = _shortconv_call(hp, hy_conv_w[l], hy_conv_b[l], n_lat_chunks, False, F32)
        hp4 = hpc.reshape(B, T // DFT2, DFT2, 3 * HY_WIDTH)
        gh4 = gh.reshape(B, T // DFT2, DFT2, HY_WIDTH)
        w1p = jnp.pad(hy_w1[l], ((0, HY_HIDDEN - HY_EMB), (0, 0)))
        filt_args = (w1p, hy_b1[l].reshape(1, -1), hy_w2[l], hy_b2[l].reshape(1, -1), hy_freq[l].reshape(1, -1),
                     hy_w3[l], hy_b3[l].reshape(1, -1), deltas)
        ncb = HY_WIDTH // HY_CB
        seqs = [(feats_lat, t_lat, tabs_lat, 0, H_lat)]
        if need_ctx:
            seqs.append((feats_ctx, t_ctx, tabs_ctx, H_lat // H_ctx, H_ctx))
        ohs = []
        for feats, tpos, tabs, rb, H in seqs:
            fa, fb = _hyfilt_call(feats, tpos, *filt_args)
            kr, ki = _hyspec_call(fa, fb, tabs)
            zz = _hyconv_call(hp4, rb, 0, hp4, rb, ncb, None, 0, kr, ki, hy_bias[l], 0, tabs, H, F32)
            oh = _hyconv_call(zz, 0, 0, hp4, rb, 2 * ncb, gh4, rb, kr, ki, hy_bias[l], 1, tabs, H, BF16)
            ohs.append(oh.reshape(B, H * DFT2, HY_WIDTH))
        oh = jnp.concatenate(ohs, axis=1) if need_ctx else ohs[0]

        hcat = _outproj_call(yg, od, oh, hcat, modcat, w_out[l].astype(BF16), ssd_norm_w[l], ln_g[l], ln_b[l],
                             n_lat_tiles, n_tiles if need_ctx else n_lat_tiles, alpha)
    return hcat[:, :L]
```

```python
import functools
import math

import numpy as np
import jax
import jax.numpy as jnp
from jax import lax
from jax.experimental import pallas as pl
from jax.experimental.pallas import tpu as pltpu

F32 = jnp.float32
BF16 = jnp.bfloat16

D_MODEL = 1024
GRID_W = 64
SSD_HEADS = 16
SSD_HEAD_DIM = 64
SSD_WIDTH = SSD_HEADS * SSD_HEAD_DIM
SSD_GROUPS = 4
SSD_HPG = SSD_HEADS // SSD_GROUPS
SSD_STATE = 128
SSD_GN = SSD_GROUPS * SSD_STATE
SSD_XBC = SSD_WIDTH + 2 * SSD_GN
SSD_DT = 2 * SSD_HEADS
SSD_CHUNK = 128
GROUP_COLS = SSD_HPG * SSD_HEAD_DIM
DIFF_HEADS = 4
DIFF_DIM = 64
DIFF_WIDTH = DIFF_HEADS * 2 * DIFF_DIM
DIFF_SCALE = DIFF_DIM ** -0.5
LOG2E = math.log2(math.e)
ROPE_BASE = 10000.0
ROPE_FREQS = DIFF_DIM // 4
HY_WIDTH = 512
HY_ORDER = 2
HY_BANDS = 16
HY_EMB = 1 + 2 * HY_BANDS
HY_HIDDEN = 64
HY_MIN_DECAY = math.log(1e-2) / 1.5
HY_MAX_DECAY = math.log(1e-2) / 0.3
IN_SIZES = (SSD_WIDTH, SSD_XBC, SSD_DT, DIFF_WIDTH, DIFF_WIDTH, DIFF_WIDTH, DIFF_WIDTH, 3 * HY_WIDTH, HY_WIDTH)
IN_OFFS = tuple(int(s) for s in np.cumsum((0,) + IN_SIZES))
LN_EPS = 1e-5
RMS_EPS = 1e-6

LANES = 128
TM = 256
TQ = 512
DFT2 = 64
SUB = 8
HY_CB = 256
HY_UNROLL = 8
VMEM_BIG = 56 * 2 ** 20
NEG_BIG = -1e30

_W_Z, _W_XBC, _W_Q, _W_QP, _W_K, _W_KP, _W_V, _W_GD, _W_HP, _W_GH, _W_DT, _W_END = (
    0, 1024, 3072, 3584, 4096, 4608, 5120, 5632, 6144, 7680, 8192, 8320)


def _cparams(sem, vmem=None):
    return pltpu.CompilerParams(dimension_semantics=sem, vmem_limit_bytes=vmem)


def _silu(x):
    return x * (1.0 / (1.0 + jnp.exp(-x)))


def _dot(a, b):
    return jnp.dot(a, b, preferred_element_type=F32)


def _split3(x):
    x1 = x.astype(BF16)
    r1 = x - x1.astype(F32)
    x2 = r1.astype(BF16)
    x3 = (r1 - x2.astype(F32)).astype(BF16)
    return x1, x2, x3


def _dot3(a, b):
    a1 = a.astype(BF16)
    a2 = (a - a1.astype(F32)).astype(BF16)
    b1 = b.astype(BF16)
    b2 = (b - b1.astype(F32)).astype(BF16)
    return _dot(a1, b1) + (_dot(a1, b2) + _dot(a2, b1))


def _ada_kernel(c_ref, w_ref, b_ref, o_ref):
    o_ref[0] = _dot3(_silu(c_ref[...]), w_ref[0]) + b_ref[0]


def _ada_call(cc, w_ada, b_ada):
    depth, d, n3 = w_ada.shape
    rows = cc.shape[0]
    nb = n3 // d
    return pl.pallas_call(
        _ada_kernel,
        out_shape=jax.ShapeDtypeStruct((depth, rows, n3), F32),
        grid=(depth, nb),
        in_specs=[pl.BlockSpec((rows, d), lambda l, j: (0, 0)),
                  pl.BlockSpec((1, d, d), lambda l, j: (l, 0, j)),
                  pl.BlockSpec((1, 1, d), lambda l, j: (l, 0, j))],
        out_specs=pl.BlockSpec((1, rows, d), lambda l, j: (l, 0, j)),
        compiler_params=_cparams(("arbitrary", "arbitrary")),
        name="ada_mod",
    )(cc, w_ada, b_ada.reshape(depth, 1, n3))


def _inproj_kernel(h_ref, mod_ref, w_ref, dtb_ref, cos_ref, sin_ref,
                   z_ref, xbc_ref, q_ref, k_ref, v_ref, gd_ref, hp_ref, gh_ref, dt_ref):
    nb, tm, d = h_ref.shape
    us = []
    for j in range(nb):
        x = h_ref[j]
        xc = x - jnp.mean(x, axis=-1, keepdims=True)
        var = jnp.mean(xc * xc, axis=-1, keepdims=True)
        xn = xc * lax.rsqrt(var + LN_EPS)
        shift = mod_ref[j, 0, :, 0:d]
        scale = mod_ref[j, 0, :, d:2 * d]
        us.append((xn * (1.0 + scale) + shift).astype(BF16))
    u = jnp.concatenate(us, axis=0) if nb > 1 else us[0]

    def mm(c0, c1):
        return _dot(u, w_ref[:, c0:c1])

    def put(ref, val):
        for j in range(nb):
            ref[j] = val[j * tm:(j + 1) * tm].astype(ref.dtype)

    put(z_ref, mm(_W_Z, _W_XBC))
    put(xbc_ref, mm(_W_XBC, _W_Q))
    reps = DIFF_WIDTH // LANES
    cos = jnp.concatenate([cos_ref[...]] * reps, axis=1)
    sin = jnp.concatenate([sin_ref[...]] * reps, axis=1)
    if nb > 1:
        cos = jnp.concatenate([cos] * nb, axis=0)
        sin = jnp.concatenate([sin] * nb, axis=0)
    put(q_ref, mm(_W_Q, _W_QP) * cos + mm(_W_QP, _W_K) * sin)
    put(k_ref, mm(_W_K, _W_KP) * cos + mm(_W_KP, _W_V) * sin)
    put(v_ref, mm(_W_V, _W_GD))
    put(gd_ref, mm(_W_GD, _W_HP))
    put(hp_ref, mm(_W_HP, _W_GH))
    put(gh_ref, mm(_W_GH, _W_DT))
    dtx = mm(_W_DT, _W_END) + dtb_ref[...]
    put(dt_ref, jnp.maximum(dtx, 0.0) + jnp.log1p(jnp.exp(-jnp.abs(dtx))))


def _inproj_call(hcat, modcat, w_cat, dtb, rope_cos, rope_sin, n_lat_tiles):
    B, T, D = hcat.shape
    nb = 2 if B % 2 == 0 else 1
    nt = T // TM

    def kind(i):
        return jnp.where(i >= n_lat_tiles, 1, 0)

    widths = (SSD_WIDTH, SSD_XBC, DIFF_WIDTH, DIFF_WIDTH, DIFF_WIDTH, DIFF_WIDTH, 3 * HY_WIDTH, HY_WIDTH, LANES)
    dtypes = (BF16,) * 8 + (F32,)
    out_shape = tuple(jax.ShapeDtypeStruct((B, T, w), dt) for w, dt in zip(widths, dtypes))
    out_specs = tuple(pl.BlockSpec((nb, TM, w), lambda b, i: (b, i, 0)) for w in widths)
    return pl.pallas_call(
        _inproj_kernel,
        out_shape=out_shape,
        grid=(B // nb, nt),
        in_specs=[pl.BlockSpec((nb, TM, D), lambda b, i: (b, i, 0)),
                  pl.BlockSpec((nb, 1, 1, 3 * D), lambda b, i: (b, kind(i), 0, 0)),
                  pl.BlockSpec((D, _W_END), lambda b, i: (0, 0), pipeline_mode=pl.Buffered(1)),
                  pl.BlockSpec((1, LANES), lambda b, i: (0, 0)),
                  pl.BlockSpec((TM, LANES), lambda b, i: (i, 0)),
                  pl.BlockSpec((TM, LANES), lambda b, i: (i, 0))],
        out_specs=out_specs,
        compiler_params=_cparams(("arbitrary", "arbitrary"), VMEM_BIG),
        name="ln_inproj",
    )(hcat, modcat, w_cat, dtb, rope_cos, rope_sin)


def _shortconv_kernel(x_ref, w_ref, b_ref, o_ref, *, n_lat_chunks, n_chunks, act):
    q = SSD_CHUNK
    pack = 16
    w0 = w_ref[0:1, :]
    w1 = w_ref[1:2, :]
    w2 = w_ref[2:3, :]
    bias = b_ref[...]
    rowid = lax.broadcasted_iota(jnp.int32, (q, 1), 0)

    def body(c, carry):
        r0 = pl.multiple_of(c * q, q)
        cur = x_ref[0, pl.ds(r0, q), :].astype(F32)
        p0 = pl.multiple_of(jnp.maximum(r0 - pack, 0), pack)
        n0 = pl.multiple_of(jnp.minimum(r0 + q, (n_chunks - 1) * q), pack)
        prow = x_ref[0, pl.ds(p0, pack), :].astype(F32)[pack - 1:pack, :]
        nrow = x_ref[0, pl.ds(n0, pack), :].astype(F32)[0:1, :]
        pvalid = jnp.logical_and(c != 0, c != n_lat_chunks)
        nvalid = jnp.logical_and(c != n_lat_chunks - 1, c != n_chunks - 1)
        prow = jnp.where(pvalid, prow, 0.0)
        nrow = jnp.where(nvalid, nrow, 0.0)
        prev = jnp.where(rowid == 0, prow, pltpu.roll(cur, 1, 0))
        nxt = jnp.where(rowid == q - 1, nrow, pltpu.roll(cur, q - 1, 0))
        y = prev * w0 + cur * w1 + nxt * w2 + bias
        if act:
            y = _silu(y)
        o_ref[0, pl.ds(r0, q), :] = y.astype(o_ref.dtype)
        return carry

    lax.fori_loop(0, n_chunks, body, 0)


def _shortconv_call(x, w, b, n_lat_chunks, act, out_dtype):
    B, T, C = x.shape
    cb = 256
    kern = functools.partial(_shortconv_kernel, n_lat_chunks=n_lat_chunks, n_chunks=T // SSD_CHUNK, act=act)
    return pl.pallas_call(
        kern,
        out_shape=jax.ShapeDtypeStruct((B, T, C), out_dtype),
        grid=(B, C // cb),
        in_specs=[pl.BlockSpec((1, T, cb), lambda b, j: (b, 0, j)),
                  pl.BlockSpec((3, cb), lambda b, j: (0, j)),
                  pl.BlockSpec((1, cb), lambda b, j: (0, j))],
        out_specs=pl.BlockSpec((1, T, cb), lambda b, j: (b, 0, j)),
        compiler_params=_cparams(("arbitrary", "arbitrary")),
        name="shortconv_act" if act else "shortconv",
    )(x, w, b.reshape(1, C))


def _ssd_kernel(x_ref, bt_ref, c_ref, z_ref, dtr_ref, alc_ref, dsk_ref, o_ref,
                yacc, ef_s, eb_s, sf_s, sb_s, df_s, db_s, hf_s, hb_s, *, n_lat_chunks, n_ctx_chunks):
    q = SSD_CHUNK
    r_heads = SSD_HPG
    gc = GROUP_COLS
    nch = n_lat_chunks + n_ctx_chunks
    li = lax.broadcasted_iota(jnp.int32, (q, q), 0)
    si = lax.broadcasted_iota(jnp.int32, (q, q), 1)
    low = si <= li
    upp = si >= li
    tri_u = jnp.where(upp, 1.0, 0.0).astype(BF16)
    a_col = -jnp.exp(alc_ref[0]) * LOG2E
    dsk = dsk_ref[0]
    is_fwd = lax.broadcasted_iota(jnp.int32, (2 * r_heads, 1), 0) < r_heads
    lane_head = lax.broadcasted_iota(jnp.int32, (1, gc), 1) // SSD_HEAD_DIM
    first_half = lax.broadcasted_iota(jnp.int32, (1, q), 1) < SSD_HEAD_DIM

    def by_head(parts):
        return jnp.concatenate([jnp.where(first_half, parts[0], parts[1]),
                                jnp.where(first_half, parts[2], parts[3])], axis=1)

    def phase_a(c, carry):
        r0 = pl.multiple_of(c * q, q)
        xh = x_ref[0, pl.ds(r0, q), :]
        cc = c_ref[0, pl.ds(r0, q), :]
        dtr = dtr_ref[0, 0, c]
        dta = dtr * a_col
        d1, d2, d3 = _split3(dta)
        pre = _dot(d1, tri_u) + (_dot(d2, tri_u) + _dot(d3, tri_u))
        tot = pre[:, q - 1:q]
        la = jnp.where(is_fwd, pre, tot - pre + dta)
        wrow = dtr * jnp.exp2(tot - la)
        dec = jnp.exp2(tot)
        bt = bt_ref[0, 0, c]
        cb = _dot(cc, bt)
        zero = jnp.zeros_like(xh)
        xm = [jnp.where(lane_head == r, xh, zero) for r in range(r_heads)]
        y = dsk * xh.astype(F32)
        for pair in range(r_heads // 2):
            efs, ebs = [], []
            for r in (2 * pair, 2 * pair + 1):
                colf = jnp.broadcast_to(la[r:r + 1, :], (q, q)).T
                colb = jnp.broadcast_to(la[r_heads + r:r_heads + r + 1, :], (q, q)).T
                wf = jnp.exp2(jnp.where(low, colf - la[r:r + 1, :], NEG_BIG)) * dtr[r:r + 1, :]
                wb = jnp.exp2(jnp.where(upp, colb - la[r_heads + r:r_heads + r + 1, :], NEG_BIG)) \
                    * dtr[r_heads + r:r_heads + r + 1, :]
                y = y + _dot(((wf + wb) * cb).astype(BF16), xm[r])
                efs.append(jnp.exp2(colf))
                ebs.append(jnp.exp2(colb))
            cols = slice(pair * q, (pair + 1) * q)
            ef_s[pl.ds(r0, q), cols] = jnp.where(first_half, efs[0], efs[1]).astype(ef_s.dtype)
            eb_s[pl.ds(r0, q), cols] = jnp.where(first_half, ebs[0], ebs[1]).astype(eb_s.dtype)
        yacc[pl.ds(r0, q), :] = y
        xst = jnp.concatenate(xm, axis=0)
        btf = bt.astype(F32)
        sts = []
        for d in range(2):
            bw = jnp.concatenate([(btf * wrow[d * r_heads + r:d * r_heads + r + 1, :]).astype(BF16)
                                  for r in range(r_heads)], axis=1)
            sts.append(_dot(bw, xst))
        st = jnp.concatenate(sts, axis=1)
        sf_s[c] = st[:, :gc]
        sb_s[c] = st[:, gc:]
        df_s[c] = jnp.broadcast_to(by_head([dec[r:r + 1, :] for r in range(r_heads)]), (SUB, gc))
        db_s[c] = jnp.broadcast_to(by_head([dec[r_heads + r:r_heads + r + 1, :] for r in range(r_heads)]),
                                   (SUB, gc))
        return carry

    lax.fori_loop(0, nch, phase_a, 0, unroll=2)

    hf_s[...] = jnp.zeros_like(hf_s)
    hb_s[...] = jnp.zeros_like(hb_s)

    def phase_b(j, carry):
        c_f = jnp.where(j < n_ctx_chunks, n_lat_chunks + j, j - n_ctx_chunks)
        c_b = nch - 1 - j
        for c, h_s, e_s, s_s, d_s in ((c_f, hf_s, ef_s, sf_s, df_s), (c_b, hb_s, eb_s, sb_s, db_s)):
            r0 = pl.multiple_of(c * q, q)
            h = h_s[...]
            ch = _dot(c_ref[0, pl.ds(r0, q), :], h.astype(BF16))
            yacc[pl.ds(r0, q), :] += ch * e_s[pl.ds(r0, q), :].astype(F32)
            h_s[...] = h * d_s[c][0:1, :] + s_s[c]
        return carry

    lax.fori_loop(0, nch, phase_b, 0, unroll=2)

    def phase_c(c, carry):
        r0 = pl.multiple_of(c * q, q)
        zz = z_ref[0, pl.ds(r0, q), :].astype(F32)
        o_ref[0, pl.ds(r0, q), :] = (yacc[pl.ds(r0, q), :] * _silu(zz)).astype(o_ref.dtype)
        return carry

    lax.fori_loop(0, nch, phase_c, 0, unroll=2)


def _ssd_call(xbc_c, bt, z, dt_row, alog_col, dsk, n_lat_chunks, n_ctx_chunks):
    B, T, _ = xbc_c.shape
    G = SSD_GROUPS
    gc = GROUP_COLS
    nch = T // SSD_CHUNK
    c_off = (SSD_WIDTH + SSD_GN) // SSD_STATE
    kern = functools.partial(_ssd_kernel, n_lat_chunks=n_lat_chunks, n_ctx_chunks=n_ctx_chunks)
    return pl.pallas_call(
        kern,
        out_shape=jax.ShapeDtypeStruct((B, T, SSD_WIDTH), BF16),
        grid=(B, G),
        in_specs=[pl.BlockSpec((1, T, gc), lambda b, g: (b, 0, g)),
                  pl.BlockSpec((1, 1, nch, SSD_STATE, SSD_CHUNK), lambda b, g: (b, g, 0, 0, 0)),
                  pl.BlockSpec((1, T, SSD_STATE), lambda b, g: (b, 0, c_off + g)),
                  pl.BlockSpec((1, T, gc), lambda b, g: (b, 0, g)),
                  pl.BlockSpec((1, 1, nch, 2 * SSD_HPG, SSD_CHUNK), lambda b, g: (b, g, 0, 0, 0)),
                  pl.BlockSpec((1, 2 * SSD_HPG, 1), lambda b, g: (g, 0, 0)),
                  pl.BlockSpec((1, 1, gc), lambda b, g: (g, 0, 0))],
        out_specs=pl.BlockSpec((1, T, gc), lambda b, g: (b, 0, g)),
        scratch_shapes=[pltpu.VMEM((T, gc), F32), pltpu.VMEM((T, gc), BF16), pltpu.VMEM((T, gc), BF16),
                        pltpu.VMEM((nch, SSD_STATE, gc), F32), pltpu.VMEM((nch, SSD_STATE, gc), F32),
                        pltpu.VMEM((nch, SUB, gc), F32), pltpu.VMEM((nch, SUB, gc), F32),
                        pltpu.VMEM((SSD_STATE, gc), F32), pltpu.VMEM((SSD_STATE, gc), F32)],
        compiler_params=_cparams(("arbitrary", "arbitrary"), VMEM_BIG),
        name="ssd",
    )(xbc_c, bt, xbc_c, z, dt_row, alog_col, dsk)


def _attn_kernel(lam_ref, q_ref, k_ref, v_ref, gd_ref, o_ref, vext, *, lam_init):
    hw = 2 * DIFF_DIM

    @pl.when(pl.program_id(2) == 0)
    def _():
        col = lax.broadcasted_iota(jnp.int32, (v_ref.shape[1], hw), 1)
        vext[:, :hw] = v_ref[0]
        vext[:, hw:] = jnp.where(col == 0, 1.0, 0.0).astype(BF16)

    lane = lax.broadcasted_iota(jnp.int32, (1, hw), 1)
    qt = q_ref[0]
    kt = k_ref[0]
    zero = jnp.zeros_like(qt)
    scores = [lax.dot_general(jnp.where((lane >= DIFF_DIM) == (m == 1), qt, zero), kt,
                              (((1,), (1,)), ((), ())), preferred_element_type=F32) for m in range(2)]
    heads = []
    for s in scores:
        p = jnp.exp2(s - jnp.max(s, axis=1, keepdims=True)).astype(BF16)
        ov = _dot(p, vext[...])
        heads.append(ov[:, :hw] / ov[:, hw:hw + 1])

    lv = lam_ref[...]
    lam = (jnp.exp(jnp.sum(lv[0:1] * lv[1:2], axis=1, keepdims=True))
           - jnp.exp(jnp.sum(lv[2:3] * lv[3:4], axis=1, keepdims=True)) + lam_init)
    o = heads[0] - lam * heads[1]
    o = o * lax.rsqrt(jnp.mean(o * o, axis=1, keepdims=True) + RMS_EPS) * (1.0 - lam_init)
    o_ref[0] = (o * _silu(gd_ref[0].astype(F32))).astype(o_ref.dtype)


def _attn_call(lam_vecs, q, k, v, gd, tq, q_tile0, n_q_tiles, kv_rows, kv_blk, lam_init):
    B = q.shape[0]
    hw = 2 * DIFF_DIM
    kern = functools.partial(_attn_kernel, lam_init=lam_init)
    return pl.pallas_call(
        kern,
        out_shape=jax.ShapeDtypeStruct((B, n_q_tiles * tq, DIFF_WIDTH), BF16),
        grid=(B, DIFF_HEADS, n_q_tiles),
        in_specs=[pl.BlockSpec((4, DIFF_DIM), lambda b, h, i: (0, 0)),
                  pl.BlockSpec((1, tq, hw), lambda b, h, i: (b, q_tile0 + i, h)),
                  pl.BlockSpec((1, kv_rows, hw), lambda b, h, i: (b, kv_blk, h)),
                  pl.BlockSpec((1, kv_rows, hw), lambda b, h, i: (b, kv_blk, h)),
                  pl.BlockSpec((1, tq, hw), lambda b, h, i: (b, q_tile0 + i, h))],
        out_specs=pl.BlockSpec((1, tq, hw), lambda b, h, i: (b, i, h)),
        scratch_shapes=[pltpu.VMEM((kv_rows, 2 * hw), BF16)],
        compiler_params=_cparams(("arbitrary", "arbitrary", "arbitrary"), VMEM_BIG),
        name="diff_attn",
    )(lam_vecs, q, k, v, gd)


def _hyfilt_kernel(feat_ref, t_ref, w1_ref, b1_ref, w2_ref, b2_ref, fr_ref, w3a_ref, b3a_ref,
                   w3b_ref, b3b_ref, dl_ref, oa_ref, ob_ref):
    fr = fr_ref[...]
    h1 = jnp.sin(fr * (_dot3(feat_ref[...], w1_ref[...]) + b1_ref[...]))
    h2 = jnp.sin(fr * (_dot3(h1, w2_ref[...]) + b2_ref[...]))
    win = jnp.exp(-t_ref[...] * dl_ref[...])
    ca = (_dot3(h2, w3a_ref[...]) + b3a_ref[...]) * win
    cb = (_dot3(h2, w3b_ref[...]) + b3b_ref[...]) * win
    rowid = lax.broadcasted_iota(jnp.int32, (ca.shape[0], 1), 0)
    cb = jnp.where(rowid == 0, 0.0, cb)
    energy = jnp.sum(ca * ca, axis=0, keepdims=True) + jnp.sum(cb * cb, axis=0, keepdims=True)
    sc = lax.rsqrt(energy + RMS_EPS)
    oa_ref[0] = ca * sc
    ob_ref[0] = cb * sc


def _hyfilt_call(feats, tpos, w1p, b1, w2, b2, freq, w3, b3, deltas):
    L = feats.shape[0]
    cb = HY_CB
    ncb = HY_WIDTH // cb
    hid = HY_HIDDEN

    def full(shape):
        return pl.BlockSpec(shape, lambda o, j: (0,) * len(shape))

    def w3spec(d):
        return pl.BlockSpec((hid, cb), lambda o, j: (0, (d * HY_ORDER + o) * ncb + j))

    def b3spec(d):
        return pl.BlockSpec((1, cb), lambda o, j: (0, (d * HY_ORDER + o) * ncb + j))

    out = jax.ShapeDtypeStruct((HY_ORDER, L, HY_WIDTH), F32)
    ospec = pl.BlockSpec((1, L, cb), lambda o, j: (o, 0, j))
    return pl.pallas_call(
        _hyfilt_kernel,
        out_shape=(out, out),
        grid=(HY_ORDER, ncb),
        in_specs=[full((L, hid)), full((L, 1)), full((hid, hid)), full((1, hid)), full((hid, hid)),
                  full((1, hid)), full((1, hid)), w3spec(0), b3spec(0), w3spec(1), b3spec(1),
                  pl.BlockSpec((1, cb), lambda o, j: (0, j))],
        out_specs=(ospec, ospec),
        compiler_params=_cparams(("arbitrary", "arbitrary"), VMEM_BIG),
        name="hy_filter",
    )(feats, tpos, w1p, b1, w2, b2, freq, w3, b3, w3, b3, deltas)


def _twiddle(twc_ref, tws_ref, k1, width):
    reps = width // LANES
    c = jnp.concatenate([twc_ref[k1]] * reps, axis=1)
    s = jnp.concatenate([tws_ref[k1]] * reps, axis=1)
    return c, s


def _hyspec_kernel(fa_ref, fb_ref, gl1_ref, twc_ref, tws_ref, f64_ref, kr_ref, ki_ref, wsr, wsi, *, k1c):
    kc = pl.program_id(2)
    cw = fa_ref.shape[-1]
    gl1 = gl1_ref[0]
    for j in range(DFT2 // SUB):
        xa = fa_ref[0, :, SUB * j:SUB * (j + 1), :]
        xb = fb_ref[0, :, SUB * j:SUB * (j + 1), :]
        h = xa.shape[0]
        xs = jnp.concatenate([xa.reshape(h * SUB, cw), xb.reshape(h * SUB, cw)], axis=1).astype(BF16)
        a = _dot(gl1, xs)
        wsr[:, SUB * j:SUB * (j + 1), :] = a[:k1c * SUB].reshape(k1c, SUB, 2 * cw)
        wsi[:, SUB * j:SUB * (j + 1), :] = a[k1c * SUB:].reshape(k1c, SUB, 2 * cw)
    f64 = f64_ref[...]

    def body(k, carry):
        c, s = _twiddle(twc_ref, tws_ref, kc * k1c + k, 2 * cw)
        ar = wsr[k]
        ai = wsi[k]
        br = ar * c + ai * s
        bi = ai * c - ar * s
        x = _dot(f64, jnp.concatenate([br, bi], axis=0).astype(BF16))
        xr = x[:DFT2]
        xi = x[DFT2:]
        r0 = pl.multiple_of(k * DFT2, DFT2)
        kr_ref[0, pl.ds(r0, DFT2), :] = (xr[:, :cw] + xr[:, cw:]).astype(kr_ref.dtype)
        ki_ref[0, pl.ds(r0, DFT2), :] = (xi[:, :cw] - xi[:, cw:]).astype(ki_ref.dtype)
        return carry

    lax.fori_loop(0, k1c, body, 0, unroll=min(HY_UNROLL, k1c))


def _hyspec_call(fa, fb, tabs):
    _, L, _ = fa.shape
    H = L // DFT2
    n1 = 2 * H
    k1c = tabs["k1c"]
    nkc = n1 // k1c
    cb = HY_CB
    ncb = HY_WIDTH // cb
    fa4 = fa.reshape(HY_ORDER, H, DFT2, HY_WIDTH)
    fb4 = fb.reshape(HY_ORDER, H, DFT2, HY_WIDTH)
    out = jax.ShapeDtypeStruct((HY_ORDER, n1 * DFT2, HY_WIDTH), BF16)
    ospec = pl.BlockSpec((1, k1c * DFT2, cb), lambda o, j, kc: (o, kc, j))
    fspec = pl.BlockSpec((1, H, DFT2, cb), lambda o, j, kc: (o, 0, 0, j))
    kern = functools.partial(_hyspec_kernel, k1c=k1c)
    return pl.pallas_call(
        kern,
        out_shape=(out, out),
        grid=(HY_ORDER, ncb, nkc),
        in_specs=[fspec, fspec,
                  pl.BlockSpec((1, 2 * k1c * SUB, H * SUB), lambda o, j, kc: (kc, 0, 0)),
                  pl.BlockSpec((n1, DFT2, LANES), lambda o, j, kc: (0, 0, 0), pipeline_mode=pl.Buffered(1)),
                  pl.BlockSpec((n1, DFT2, LANES), lambda o, j, kc: (0, 0, 0), pipeline_mode=pl.Buffered(1)),
                  pl.BlockSpec((2 * DFT2, 2 * DFT2), lambda o, j, kc: (0, 0))],
        out_specs=(ospec, ospec),
        scratch_shapes=[pltpu.VMEM((k1c, DFT2, 2 * cb), F32), pltpu.VMEM((k1c, DFT2, 2 * cb), F32)],
        compiler_params=_cparams(("arbitrary", "arbitrary", "arbitrary"), VMEM_BIG),
        name="hy_spectrum",
    )(fa4, fb4, tabs["gl1"], tabs["twc"], tabs["tws"], tabs["f64"])


def _hyconv_kernel(*refs, k1c, nkc, n_fft, final):
    u_ref, g_ref = refs[0], refs[1]
    pos = 2
    gh_ref = None
    if final:
        gh_ref = refs[pos]
        pos += 1
    kr_ref, ki_ref, bias_ref, gl1_ref, gl1i_ref, twc_ref, tws_ref, f64_ref, f64i_ref = refs[pos:pos + 9]
    o_ref, wsr, wsi, acc = refs[pos + 9:pos + 13]
    kc = pl.program_id(2)
    cw = u_ref.shape[-1]
    h = u_ref.shape[1]

    gl1 = gl1_ref[0]
    for j in range(DFT2 // SUB):
        xs = u_ref[0, :, SUB * j:SUB * (j + 1), :].reshape(h * SUB, cw).astype(BF16)
        a = _dot(gl1, xs)
        wsr[:, SUB * j:SUB * (j + 1), :] = a[:k1c * SUB].reshape(k1c, SUB, cw)
        wsi[:, SUB * j:SUB * (j + 1), :] = a[k1c * SUB:].reshape(k1c, SUB, cw)
    f64 = f64_ref[...]
    f64i = f64i_ref[...]

    def body(k, carry):
        c, s = _twiddle(twc_ref, tws_ref, kc * k1c + k, cw)
        ar = wsr[k]
        ai = wsi[k]
        br = ar * c + ai * s
        bi = ai * c - ar * s
        x = _dot(f64, jnp.concatenate([br, bi], axis=0).astype(BF16))
        xr = x[:DFT2]
        xi = x[DFT2:]
        r0 = pl.multiple_of(k * DFT2, DFT2)
        kr = kr_ref[0, pl.ds(r0, DFT2), :].astype(F32)
        ki = ki_ref[0, pl.ds(r0, DFT2), :].astype(F32)
        yr = xr * kr - xi * ki
        yi = xr * ki + xi * kr
        cc = _dot(f64i, jnp.concatenate([yr, yi], axis=0).astype(BF16))
        cr = cc[:DFT2]
        ci = cc[DFT2:]
        wsr[k] = cr * c - ci * s
        wsi[k] = ci * c + cr * s
        return carry

    lax.fori_loop(0, k1c, body, 0, unroll=min(HY_UNROLL, k1c))

    @pl.when(kc == 0)
    def _():
        acc[...] = jnp.zeros_like(acc)

    gl1i = gl1i_ref[0]
    for j in range(DFT2 // SUB):
        dr = wsr[:, SUB * j:SUB * (j + 1), :].reshape(k1c * SUB, cw)
        di = wsi[:, SUB * j:SUB * (j + 1), :].reshape(k1c * SUB, cw)
        yj = _dot(gl1i, jnp.concatenate([dr, di], axis=0).astype(BF16))
        acc[:, SUB * j:SUB * (j + 1), :] += yj.reshape(h, SUB, cw)

    @pl.when(kc == nkc - 1)
    def _():
        u = u_ref[0]
        y = g_ref[0] * (acc[...] * (1.0 / n_fft) + bias_ref[0] * u)
        if final:
            y = y * _silu(gh_ref[0].astype(F32))
        o_ref[0] = y.astype(o_ref.dtype)


def _hyconv_call(u4, u_rb, u_cb, g4, g_rb, g_cb, gh4, gh_rb, kr, ki, bias, order, tabs, H, out_dtype):
    B = u4.shape[0]
    k1c = tabs["k1c"]
    n1 = 2 * H
    nkc = n1 // k1c
    cb = HY_CB
    final = gh4 is not None

    def seq_spec(rb, off):
        return pl.BlockSpec((1, H, DFT2, cb), lambda b, j, kc: (b, rb, 0, off + j))

    in_specs = [seq_spec(u_rb, u_cb), seq_spec(g_rb, g_cb)]
    args = [u4, g4]
    if final:
        in_specs.append(seq_spec(gh_rb, 0))
        args.append(gh4)
    in_specs += [
        pl.BlockSpec((1, k1c * DFT2, cb), lambda b, j, kc: (order, kc, j)),
        pl.BlockSpec((1, k1c * DFT2, cb), lambda b, j, kc: (order, kc, j)),
        pl.BlockSpec((1, 1, cb), lambda b, j, kc: (order, 0, j)),
        pl.BlockSpec((1, 2 * k1c * SUB, H * SUB), lambda b, j, kc: (kc, 0, 0)),
        pl.BlockSpec((1, H * SUB, 2 * k1c * SUB), lambda b, j, kc: (kc, 0, 0)),
        pl.BlockSpec((n1, DFT2, LANES), lambda b, j, kc: (0, 0, 0), pipeline_mode=pl.Buffered(1)),
        pl.BlockSpec((n1, DFT2, LANES), lambda b, j, kc: (0, 0, 0), pipeline_mode=pl.Buffered(1)),
        pl.BlockSpec((2 * DFT2, 2 * DFT2), lambda b, j, kc: (0, 0)),
        pl.BlockSpec((2 * DFT2, 2 * DFT2), lambda b, j, kc: (0, 0)),
    ]
    args += [kr, ki, bias.reshape(HY_ORDER, 1, HY_WIDTH), tabs["gl1"], tabs["gl1i"], tabs["twc"], tabs["tws"],
             tabs["f64"], tabs["f64i"]]
    kern = functools.partial(_hyconv_kernel, k1c=k1c, nkc=nkc, n_fft=float(n1 * DFT2), final=final)
    return pl.pallas_call(
        kern,
        out_shape=jax.ShapeDtypeStruct((B, H, DFT2, HY_WIDTH), out_dtype),
        grid=(B, HY_WIDTH // cb, nkc),
        in_specs=in_specs,
        out_specs=pl.BlockSpec((1, H, DFT2, cb), lambda b, j, kc: (b, 0, 0, j)),
        scratch_shapes=[pltpu.VMEM((k1c, DFT2, cb), F32), pltpu.VMEM((k1c, DFT2, cb), F32),
                        pltpu.VMEM((H, DFT2, cb), F32)],
        compiler_params=_cparams(("arbitrary", "arbitrary", "arbitrary"), VMEM_BIG),
        name="hy_conv",
    )(*args)


def _dft_tables(H):
    n1 = 2 * H
    n = n1 * DFT2
    k1c = min(n1, 32)
    nkc = n1 // k1c
    k1 = np.arange(n1, dtype=np.float64)
    th = 2.0 * np.pi * np.outer(k1, np.arange(H, dtype=np.float64)) / n1
    eye = np.eye(SUB)
    gre = np.kron(np.cos(th), eye).reshape(nkc, k1c * SUB, H * SUB)
    gim = np.kron(-np.sin(th), eye).reshape(nkc, k1c * SUB, H * SUB)
    gl1 = np.concatenate([gre, gim], axis=1)
    gire = np.kron(np.cos(th).T, eye).reshape(H * SUB, nkc, k1c * SUB).transpose(1, 0, 2)
    giim = np.kron(-np.sin(th).T, eye).reshape(H * SUB, nkc, k1c * SUB).transpose(1, 0, 2)
    gl1i = np.concatenate([gire, giim], axis=2)
    ph = 2.0 * np.pi * np.outer(k1, np.arange(DFT2, dtype=np.float64)) / n
    twc = np.repeat(np.cos(ph)[:, :, None], LANES, axis=2)
    tws = np.repeat(np.sin(ph)[:, :, None], LANES, axis=2)
    ps = 2.0 * np.pi * np.outer(np.arange(DFT2), np.arange(DFT2)) / DFT2
    c64, s64 = np.cos(ps), np.sin(ps)
    f64 = np.block([[c64, s64], [-s64, c64]])
    f64i = np.block([[c64, -s64], [s64, c64]])
    as_bf = lambda a: jnp.asarray(a, F32).astype(BF16)
    return dict(k1c=k1c, gl1=as_bf(gl1), gl1i=as_bf(gl1i), twc=jnp.asarray(twc, F32),
                tws=jnp.asarray(tws, F32), f64=as_bf(f64), f64i=as_bf(f64i))


def _hy_features(L):
    t = np.linspace(0.0, 1.0, L)[:, None]
    wpos = (2.0 * math.pi / L) * np.arange(L, dtype=np.float64)[:, None]
    bands = np.linspace(1e-4, HY_BANDS - 1, HY_BANDS)
    feats = np.concatenate([t, np.cos(wpos * bands), -np.sin(wpos * bands)], -1)
    feats = np.pad(feats, ((0, 0), (0, HY_HIDDEN - HY_EMB)))
    return jnp.asarray(feats, F32), jnp.asarray(t, F32)


def _outproj_kernel(y_ref, od_ref, oh_ref, h_ref, mod_ref, w_ref, nw_ref, g_ref, b_ref, o_ref, *, alpha):
    nb, tm, d = h_ref.shape
    sw = y_ref.shape[-1]
    dw = od_ref.shape[-1]
    ys, ods, ohs = [], [], []
    for j in range(nb):
        y = y_ref[j].astype(F32)
        y = y * lax.rsqrt(jnp.mean(y * y, axis=-1, keepdims=True) + RMS_EPS) * nw_ref[...]
        ys.append(y.astype(BF16))
        ods.append(od_ref[j])
        ohs.append(oh_ref[j])
    cat = (lambda xs: jnp.concatenate(xs, axis=0)) if nb > 1 else (lambda xs: xs[0])
    o = (_dot(cat(ys), w_ref[0:sw, :]) + _dot(cat(ods), w_ref[sw:sw + dw, :])
         + _dot(cat(ohs), w_ref[sw + dw:, :]))
    for j in range(nb):
        gate = mod_ref[j, 0, :, 2 * d:3 * d]
        r = alpha * h_ref[j] + gate * o[j * tm:(j + 1) * tm]
        rc = r - jnp.mean(r, axis=-1, keepdims=True)
        var = jnp.mean(rc * rc, axis=-1, keepdims=True)
        o_ref[j] = rc * lax.rsqrt(var + LN_EPS) * g_ref[...] + b_ref[...]


def _outproj_call(yg, od, oh, hcat, modcat, w_out, norm_w, ln_g, ln_b, n_lat_tiles, n_tiles, alpha):
    B, T, D = hcat.shape
    nb = 2 if B % 2 == 0 else 1

    def kind(i):
        return jnp.where(i >= n_lat_tiles, 1, 0)

    def tok(w):
        return pl.BlockSpec((nb, TM, w), lambda b, i: (b, i, 0))

    def vec(w):
        return pl.BlockSpec((1, w), lambda b, i: (0, 0))

    kern = functools.partial(_outproj_kernel, alpha=alpha)
    return pl.pallas_call(
        kern,
        out_shape=jax.ShapeDtypeStruct((B, n_tiles * TM, D), F32),
        grid=(B // nb, n_tiles),
        in_specs=[tok(SSD_WIDTH), tok(DIFF_WIDTH), tok(HY_WIDTH), tok(D),
                  pl.BlockSpec((nb, 1, 1, 3 * D), lambda b, i: (b, kind(i), 0, 0)),
                  pl.BlockSpec(w_out.shape, lambda b, i: (0, 0)),
                  vec(SSD_WIDTH), vec(D), vec(D)],
        out_specs=tok(D),
        compiler_params=_cparams(("arbitrary", "arbitrary"), VMEM_BIG),
        name="outproj",
    )(yg, od, oh, hcat, modcat, w_out, norm_w.reshape(1, -1), ln_g.reshape(1, -1), ln_b.reshape(1, -1))


def _rope_tables(L, Lc):
    rows = np.repeat(np.arange(L // GRID_W), GRID_W).astype(np.float64)
    cols = np.tile(np.arange(GRID_W), L // GRID_W).astype(np.float64)
    inv = ROPE_BASE ** (-np.arange(ROPE_FREQS, dtype=np.float64) / ROPE_FREQS)
    lane = np.arange(LANES)
    f = lane % ROPE_FREQS
    is_row = (lane % DIFF_DIM) < (DIFF_DIM // 2)
    ang = np.where(is_row[None, :], rows[:, None] * inv[f][None, :], cols[:, None] * inv[f][None, :])
    cos = np.concatenate([np.cos(ang), np.ones((Lc, LANES))], 0)
    sin = np.concatenate([np.sin(ang), np.zeros((Lc, LANES))], 0)
    return jnp.asarray(cos, F32), jnp.asarray(sin, F32)


def _rot_partner_cols(w):
    n = w.shape[1]
    col = np.arange(n)
    first = (col % (2 * ROPE_FREQS)) < ROPE_FREQS
    src = np.where(first, col + ROPE_FREQS, col - ROPE_FREQS)
    sign = np.where(first, -1.0, 1.0).astype(np.float32)
    return w[:, src] * sign


def _dt_perm():
    g, d, r = np.meshgrid(np.arange(SSD_GROUPS), np.arange(2), np.arange(SSD_HPG), indexing="ij")
    return (d * SSD_HEADS + g * SSD_HPG + r).reshape(-1)


def _layout_w_in(w):
    o = IN_OFFS
    z, xbc, dt = w[:, o[0]:o[1]], w[:, o[1]:o[2]], w[:, o[2]:o[3]]
    q, k, v, gd = w[:, o[3]:o[4]] * (DIFF_SCALE * LOG2E), w[:, o[4]:o[5]], w[:, o[5]:o[6]], w[:, o[6]:o[7]]
    hp, gh = w[:, o[7]:o[8]], w[:, o[8]:o[9]]
    dtp = jnp.pad(dt[:, _dt_perm()], ((0, 0), (0, LANES - SSD_DT)))
    cat = jnp.concatenate([z, xbc, q, _rot_partner_cols(q), k, _rot_partner_cols(k), v, gd, hp, gh, dtp], axis=1)
    return cat.astype(BF16)


def kernel(x, c, ctx, c_ctx, w_ada, b_ada, w_in, ssd_conv_w, ssd_conv_b, ssd_dt_bias, ssd_a_log, ssd_d,
           ssd_norm_w, diff_lambda, hy_conv_w, hy_conv_b, hy_w1, hy_b1, hy_w2, hy_b2, hy_w3, hy_b3,
           hy_freq, hy_bias, w_out, ln_g, ln_b):
    B, L, D = x.shape
    Lc = ctx.shape[1]
    depth = w_in.shape[0]
    T = L + Lc
    assert L % TM == 0 and Lc % TM == 0 and L % GRID_W == 0 and D == D_MODEL
    n_lat_tiles, n_tiles = L // TM, T // TM
    n_lat_chunks, n_ctx_chunks = L // SSD_CHUNK, Lc // SSD_CHUNK
    nch = n_lat_chunks + n_ctx_chunks
    G, R8 = SSD_GROUPS, 2 * SSD_HPG
    alpha = (2 * depth) ** 0.25
    H_lat, H_ctx = L // DFT2, Lc // DFT2
    assert H_lat % H_ctx == 0 and L % Lc == 0

    rope_cos, rope_sin = _rope_tables(L, Lc)
    tabs_lat, tabs_ctx = _dft_tables(H_lat), _dft_tables(H_ctx)
    feats_lat, t_lat = _hy_features(L)
    feats_ctx, t_ctx = _hy_features(Lc)
    deltas = jnp.asarray(np.abs(np.linspace(HY_MIN_DECAY, HY_MAX_DECAY, HY_WIDTH))[None, :], F32)

    rows = -(-(B + 1) // SUB) * SUB
    cc = jnp.zeros((rows, D), F32).at[:B].set(c).at[B].set(c_ctx)
    mod_all = _ada_call(cc, w_ada, b_ada)

    hcat = jnp.concatenate([x, ctx], axis=1)
    perm = _dt_perm()
    for l in range(depth):
        need_ctx = l < depth - 1
        mod = mod_all[l]
        modcat = jnp.stack([mod[:B], jnp.broadcast_to(mod[B], (B, 3 * D))], axis=1)[:, :, None, :]
        dtb = jnp.pad(ssd_dt_bias[l].reshape(-1)[perm], (0, LANES - SSD_DT)).reshape(1, LANES)
        z, xbc, q, k, v, gd, hp, gh, dt = _inproj_call(
            hcat, modcat, _layout_w_in(w_in[l]), dtb, rope_cos, rope_sin, n_lat_tiles)

        xbc_c = _shortconv_call(xbc, ssd_conv_w[l], ssd_conv_b[l], n_lat_chunks, True, BF16)
        dt_row = dt[:, :, :SSD_DT].reshape(B, nch, SSD_CHUNK, G, R8).transpose(0, 3, 1, 4, 2)
        alog = ssd_a_log[l].reshape(2, G, SSD_HPG).transpose(1, 0, 2).reshape(G, R8)
        dsk = jnp.repeat(ssd_d[l].reshape(G, SSD_HPG), SSD_HEAD_DIM, axis=1).reshape(G, 1, GROUP_COLS)
        bt = xbc_c[:, :, SSD_WIDTH:SSD_WIDTH + SSD_GN].reshape(B, nch, SSD_CHUNK, G, SSD_STATE)
        bt = bt.transpose(0, 3, 1, 4, 2)
        yg = _ssd_call(xbc_c, bt, z, dt_row, alog.reshape(G, R8, 1), dsk, n_lat_chunks, n_ctx_chunks)

        lam_init = 0.8 - 0.6 * math.exp(-0.3 * l)
        tq = math.gcd(TQ, L)
        od = _attn_call(diff_lambda[l], q, k, v, gd, tq, 0, L // tq, T, 0, lam_init)
        if need_ctx:
            tqc = math.gcd(tq, Lc)
            od_c = _attn_call(diff_lambda[l], q, k, v, gd, tqc, L // tqc, Lc // tqc, Lc, L // Lc, lam_init)
            od = jnp.concatenate([od, od_c], axis=1)

        hpc = _shortconv_call(hp, hy_conv_w[l], hy_conv_b[l], n_lat_chunks, False, F32)
        hp4 = hpc.reshape(B, T // DFT2, DFT2, 3 * HY_WIDTH)
        gh4 = gh.reshape(B, T // DFT2, DFT2, HY_WIDTH)
        w1p = jnp.pad(hy_w1[l], ((0, HY_HIDDEN - HY_EMB), (0, 0)))
        filt_args = (w1p, hy_b1[l].reshape(1, -1), hy_w2[l], hy_b2[l].reshape(1, -1), hy_freq[l].reshape(1, -1),
                     hy_w3[l], hy_b3[l].reshape(1, -1), deltas)
        ncb = HY_WIDTH // HY_CB
        seqs = [(feats_lat, t_lat, tabs_lat, 0, H_lat)]
        if need_ctx:
            seqs.append((feats_ctx, t_ctx, tabs_ctx, H_lat // H_ctx, H_ctx))
        ohs = []
        for feats, tpos, tabs, rb, H in seqs:
            fa, fb = _hyfilt_call(feats, tpos, *filt_args)
            kr, ki = _hyspec_call(fa, fb, tabs)
            zz = _hyconv_call(hp4, rb, 0, hp4, rb, ncb, None, 0, kr, ki, hy_bias[l], 0, tabs, H, F32)
            oh = _hyconv_call(zz, 0, 0, hp4, rb, 2 * ncb, gh4, rb, kr, ki, hy_bias[l], 1, tabs, H, BF16)
            ohs.append(oh.reshape(B, H * DFT2, HY_WIDTH))
        oh = jnp.concatenate(ohs, axis=1) if need_ctx else ohs[0]

        hcat = _outproj_call(yg, od, oh, hcat, modcat, w_out[l].astype(BF16), ssd_norm_w[l], ln_g[l], ln_b[l],
                             n_lat_tiles, n_tiles if need_ctx else n_lat_tiles, alpha)
    return hcat[:, :L]
```

```python
import functools
import math

import numpy as np
import jax
import jax.numpy as jnp
from jax import lax
from jax.experimental import pallas as pl
from jax.experimental.pallas import tpu as pltpu

F32 = jnp.float32
BF16 = jnp.bfloat16

D_MODEL = 1024
GRID_W = 64
SSD_HEADS = 16
SSD_HEAD_DIM = 64
SSD_WIDTH = SSD_HEADS * SSD_HEAD_DIM
SSD_GROUPS = 4
SSD_HPG = SSD_HEADS // SSD_GROUPS
SSD_STATE = 128
SSD_GN = SSD_GROUPS * SSD_STATE
SSD_XBC = SSD_WIDTH + 2 * SSD_GN
SSD_DT = 2 * SSD_HEADS
SSD_CHUNK = 128
GROUP_COLS = SSD_HPG * SSD_HEAD_DIM
DIFF_HEADS = 4
DIFF_DIM = 64
DIFF_WIDTH = DIFF_HEADS * 2 * DIFF_DIM
DIFF_SCALE = DIFF_DIM ** -0.5
LOG2E = math.log2(math.e)
ROPE_BASE = 10000.0
ROPE_FREQS = DIFF_DIM // 4
HY_WIDTH = 512
HY_ORDER = 2
HY_BANDS = 16
HY_EMB = 1 + 2 * HY_BANDS
HY_HIDDEN = 64
HY_MIN_DECAY = math.log(1e-2) / 1.5
HY_MAX_DECAY = math.log(1e-2) / 0.3
IN_SIZES = (SSD_WIDTH, SSD_XBC, SSD_DT, DIFF_WIDTH, DIFF_WIDTH, DIFF_WIDTH, DIFF_WIDTH, 3 * HY_WIDTH, HY_WIDTH)
IN_OFFS = tuple(int(s) for s in np.cumsum((0,) + IN_SIZES))
LN_EPS = 1e-5
RMS_EPS = 1e-6

LANES = 128
TM = 256
TQ = 512
DFT2 = 64
SUB = 8
HY_CB = 256
HY_UNROLL = 6
HY_K1C = 36
VMEM_BIG = 56 * 2 ** 20
NEG_BIG = -1e30

_W_Z, _W_XBC, _W_Q, _W_K, _W_V, _W_GD, _W_HP, _W_GH, _W_DT, _W_END = (
    0, 1024, 3072, 3584, 4096, 4608, 5120, 6656, 7168, 7296)


def _cparams(sem, vmem=None):
    return pltpu.CompilerParams(dimension_semantics=sem, vmem_limit_bytes=vmem)


def _silu(x):
    return x * (1.0 / (1.0 + jnp.exp(-x)))


def _dot(a, b):
    return jnp.dot(a, b, preferred_element_type=F32)


def _split3(x):
    x1 = x.astype(BF16)
    r1 = x - x1.astype(F32)
    x2 = r1.astype(BF16)
    x3 = (r1 - x2.astype(F32)).astype(BF16)
    return x1, x2, x3


def _dot3(a, b):
    a1 = a.astype(BF16)
    a2 = (a - a1.astype(F32)).astype(BF16)
    b1 = b.astype(BF16)
    b2 = (b - b1.astype(F32)).astype(BF16)
    return _dot(a1, b1) + (_dot(a1, b2) + _dot(a2, b1))


def _ada_kernel(c_ref, w_ref, b_ref, o_ref):
    o_ref[0] = _dot3(_silu(c_ref[...]), w_ref[0]) + b_ref[0]


def _ada_call(cc, w_ada, b_ada):
    depth, d, n3 = w_ada.shape
    rows = cc.shape[0]
    nb = n3 // d
    return pl.pallas_call(
        _ada_kernel,
        out_shape=jax.ShapeDtypeStruct((depth, rows, n3), F32),
        grid=(depth, nb),
        in_specs=[pl.BlockSpec((rows, d), lambda l, j: (0, 0)),
                  pl.BlockSpec((1, d, d), lambda l, j: (l, 0, j)),
                  pl.BlockSpec((1, 1, d), lambda l, j: (l, 0, j))],
        out_specs=pl.BlockSpec((1, rows, d), lambda l, j: (l, 0, j)),
        compiler_params=_cparams(("arbitrary", "arbitrary")),
        name="ada_mod",
    )(cc, w_ada, b_ada.reshape(depth, 1, n3))


def _inproj_kernel(h_ref, mod_ref, w_ref, dtb_ref, cos_ref, sin_ref,
                   z_ref, xbc_ref, q_ref, k_ref, v_ref, gd_ref, hp_ref, gh_ref, dt_ref):
    nb, tm, d = h_ref.shape
    us = []
    for j in range(nb):
        x = h_ref[j]
        xc = x - jnp.mean(x, axis=-1, keepdims=True)
        var = jnp.mean(xc * xc, axis=-1, keepdims=True)
        xn = xc * lax.rsqrt(var + LN_EPS)
        shift = mod_ref[j, 0, :, 0:d]
        scale = mod_ref[j, 0, :, d:2 * d]
        us.append((xn * (1.0 + scale) + shift).astype(BF16))
    u = jnp.concatenate(us, axis=0) if nb > 1 else us[0]

    def mm(c0, c1):
        return _dot(u, w_ref[:, c0:c1])

    def put(ref, val):
        for j in range(nb):
            ref[j] = val[j * tm:(j + 1) * tm].astype(ref.dtype)

    put(z_ref, mm(_W_Z, _W_XBC))
    put(xbc_ref, mm(_W_XBC, _W_Q))
    reps = DIFF_WIDTH // LANES
    cos = jnp.concatenate([cos_ref[...]] * reps, axis=1)
    sin = jnp.concatenate([sin_ref[...]] * reps, axis=1)
    if nb > 1:
        cos = jnp.concatenate([cos] * nb, axis=0)
        sin = jnp.concatenate([sin] * nb, axis=0)
    lane = lax.broadcasted_iota(jnp.int32, (1, DIFF_WIDTH), 1)
    first = (lane % (2 * ROPE_FREQS)) < ROPE_FREQS

    def rope(t):
        partner = jnp.where(first, -pltpu.roll(t, DIFF_WIDTH - ROPE_FREQS, 1), pltpu.roll(t, ROPE_FREQS, 1))
        return t * cos + partner * sin

    put(q_ref, rope(mm(_W_Q, _W_K)))
    put(k_ref, rope(mm(_W_K, _W_V)))
    put(v_ref, mm(_W_V, _W_GD))
    put(gd_ref, mm(_W_GD, _W_HP))
    put(hp_ref, mm(_W_HP, _W_GH))
    put(gh_ref, mm(_W_GH, _W_DT))
    dtx = mm(_W_DT, _W_END) + dtb_ref[...]
    put(dt_ref, jnp.maximum(dtx, 0.0) + jnp.log1p(jnp.exp(-jnp.abs(dtx))))


def _inproj_call(hcat, modcat, w_cat, dtb, rope_cos, rope_sin, n_lat_tiles):
    B, T, D = hcat.shape
    nb = 2 if B % 2 == 0 else 1
    nt = T // TM

    def kind(i):
        return jnp.where(i >= n_lat_tiles, 1, 0)

    widths = (SSD_WIDTH, SSD_XBC, DIFF_WIDTH, DIFF_WIDTH, DIFF_WIDTH, DIFF_WIDTH, 3 * HY_WIDTH, HY_WIDTH, LANES)
    dtypes = (BF16,) * 8 + (F32,)
    out_shape = tuple(jax.ShapeDtypeStruct((B, T, w), dt) for w, dt in zip(widths, dtypes))
    out_specs = tuple(pl.BlockSpec((nb, TM, w), lambda b, i: (b, i, 0)) for w in widths)
    return pl.pallas_call(
        _inproj_kernel,
        out_shape=out_shape,
        grid=(B // nb, nt),
        in_specs=[pl.BlockSpec((nb, TM, D), lambda b, i: (b, i, 0)),
                  pl.BlockSpec((nb, 1, 1, 3 * D), lambda b, i: (b, kind(i), 0, 0)),
                  pl.BlockSpec((D, _W_END), lambda b, i: (0, 0), pipeline_mode=pl.Buffered(1)),
                  pl.BlockSpec((1, LANES), lambda b, i: (0, 0)),
                  pl.BlockSpec((TM, LANES), lambda b, i: (i, 0)),
                  pl.BlockSpec((TM, LANES), lambda b, i: (i, 0))],
        out_specs=out_specs,
        compiler_params=_cparams(("arbitrary", "arbitrary"), VMEM_BIG),
        name="ln_inproj",
    )(hcat, modcat, w_cat, dtb, rope_cos, rope_sin)


def _shortconv_kernel(x_ref, w_ref, b_ref, o_ref, *, n_lat_chunks, n_chunks, act):
    q = SSD_CHUNK
    pack = 16
    w0 = w_ref[0:1, :]
    w1 = w_ref[1:2, :]
    w2 = w_ref[2:3, :]
    bias = b_ref[...]
    rowid = lax.broadcasted_iota(jnp.int32, (q, 1), 0)

    def body(c, carry):
        r0 = pl.multiple_of(c * q, q)
        cur = x_ref[0, pl.ds(r0, q), :].astype(F32)
        p0 = pl.multiple_of(jnp.maximum(r0 - pack, 0), pack)
        n0 = pl.multiple_of(jnp.minimum(r0 + q, (n_chunks - 1) * q), pack)
        prow = x_ref[0, pl.ds(p0, pack), :].astype(F32)[pack - 1:pack, :]
        nrow = x_ref[0, pl.ds(n0, pack), :].astype(F32)[0:1, :]
        pvalid = jnp.logical_and(c != 0, c != n_lat_chunks)
        nvalid = jnp.logical_and(c != n_lat_chunks - 1, c != n_chunks - 1)
        prow = jnp.where(pvalid, prow, 0.0)
        nrow = jnp.where(nvalid, nrow, 0.0)
        prev = jnp.where(rowid == 0, prow, pltpu.roll(cur, 1, 0))
        nxt = jnp.where(rowid == q - 1, nrow, pltpu.roll(cur, q - 1, 0))
        y = prev * w0 + cur * w1 + nxt * w2 + bias
        if act:
            y = _silu(y)
        o_ref[0, pl.ds(r0, q), :] = y.astype(o_ref.dtype)
        return carry

    lax.fori_loop(0, n_chunks, body, 0)


def _shortconv_call(x, w, b, n_lat_chunks, act, out_dtype):
    B, T, C = x.shape
    cb = 256
    kern = functools.partial(_shortconv_kernel, n_lat_chunks=n_lat_chunks, n_chunks=T // SSD_CHUNK, act=act)
    return pl.pallas_call(
        kern,
        out_shape=jax.ShapeDtypeStruct((B, T, C), out_dtype),
        grid=(B, C // cb),
        in_specs=[pl.BlockSpec((1, T, cb), lambda b, j: (b, 0, j)),
                  pl.BlockSpec((3, cb), lambda b, j: (0, j)),
                  pl.BlockSpec((1, cb), lambda b, j: (0, j))],
        out_specs=pl.BlockSpec((1, T, cb), lambda b, j: (b, 0, j)),
        compiler_params=_cparams(("arbitrary", "arbitrary")),
        name="shortconv_act" if act else "shortconv",
    )(x, w, b.reshape(1, C))


def _ssd_kernel(x_ref, bt_ref, c_ref, z_ref, dtr_ref, alc_ref, dsk_ref, o_ref,
                yacc, ef_s, eb_s, sf_s, sb_s, df_s, db_s, hf_s, hb_s, *, n_lat_chunks, n_ctx_chunks):
    q = SSD_CHUNK
    r_heads = SSD_HPG
    gc = GROUP_COLS
    nch = n_lat_chunks + n_ctx_chunks
    li = lax.broadcasted_iota(jnp.int32, (q, q), 0)
    si = lax.broadcasted_iota(jnp.int32, (q, q), 1)
    low = si <= li
    upp = si >= li
    tri_u = jnp.where(upp, 1.0, 0.0).astype(BF16)
    a_col = -jnp.exp(alc_ref[0]) * LOG2E
    dsk = dsk_ref[0]
    is_fwd = lax.broadcasted_iota(jnp.int32, (2 * r_heads, 1), 0) < r_heads
    lane_head = lax.broadcasted_iota(jnp.int32, (1, gc), 1) // SSD_HEAD_DIM
    first_half = lax.broadcasted_iota(jnp.int32, (1, q), 1) < SSD_HEAD_DIM

    def by_head(parts):
        return jnp.concatenate([jnp.where(first_half, parts[0], parts[1]),
                                jnp.where(first_half, parts[2], parts[3])], axis=1)

    def phase_a(c, carry):
        r0 = pl.multiple_of(c * q, q)
        xh = x_ref[0, pl.ds(r0, q), :]
        cc = c_ref[0, pl.ds(r0, q), :]
        dtr = dtr_ref[0, 0, c]
        dta = dtr * a_col
        d1, d2, d3 = _split3(dta)
        pre = _dot(d1, tri_u) + (_dot(d2, tri_u) + _dot(d3, tri_u))
        tot = pre[:, q - 1:q]
        la = jnp.where(is_fwd, pre, tot - pre + dta)
        wrow = dtr * jnp.exp2(tot - la)
        dec = jnp.exp2(tot)
        bt = bt_ref[0, 0, c]
        cb = _dot(cc, bt)
        zero = jnp.zeros_like(xh)
        xm = [jnp.where(lane_head == r, xh, zero) for r in range(r_heads)]
        y = dsk * xh.astype(F32)
        for pair in range(r_heads // 2):
            efs, ebs = [], []
            for r in (2 * pair, 2 * pair + 1):
                colf = jnp.broadcast_to(la[r:r + 1, :], (q, q)).T
                colb = jnp.broadcast_to(la[r_heads + r:r_heads + r + 1, :], (q, q)).T
                wf = jnp.exp2(jnp.where(low, colf - la[r:r + 1, :], NEG_BIG)) * dtr[r:r + 1, :]
                wb = jnp.exp2(jnp.where(upp, colb - la[r_heads + r:r_heads + r + 1, :], NEG_BIG)) \
                    * dtr[r_heads + r:r_heads + r + 1, :]
                y = y + _dot(((wf + wb) * cb).astype(BF16), xm[r])
                efs.append(jnp.exp2(colf))
                ebs.append(jnp.exp2(colb))
            cols = slice(pair * q, (pair + 1) * q)
            ef_s[pl.ds(r0, q), cols] = jnp.where(first_half, efs[0], efs[1]).astype(ef_s.dtype)
            eb_s[pl.ds(r0, q), cols] = jnp.where(first_half, ebs[0], ebs[1]).astype(eb_s.dtype)
        yacc[pl.ds(r0, q), :] = y
        xst = jnp.concatenate(xm, axis=0)
        btf = bt.astype(F32)
        sts = []
        for d in range(2):
            bw = jnp.concatenate([(btf * wrow[d * r_heads + r:d * r_heads + r + 1, :]).astype(BF16)
                                  for r in range(r_heads)], axis=1)
            sts.append(_dot(bw, xst))
        st = jnp.concatenate(sts, axis=1)
        sf_s[c] = st[:, :gc]
        sb_s[c] = st[:, gc:]
        df_s[c] = jnp.broadcast_to(by_head([dec[r:r + 1, :] for r in range(r_heads)]), (SUB, gc))
        db_s[c] = jnp.broadcast_to(by_head([dec[r_heads + r:r_heads + r + 1, :] for r in range(r_heads)]),
                                   (SUB, gc))
        return carry

    lax.fori_loop(0, nch, phase_a, 0, unroll=2)

    hf_s[...] = jnp.zeros_like(hf_s)
    hb_s[...] = jnp.zeros_like(hb_s)

    def phase_b(j, carry):
        c_f = jnp.where(j < n_ctx_chunks, n_lat_chunks + j, j - n_ctx_chunks)
        c_b = nch - 1 - j
        for c, h_s, e_s, s_s, d_s in ((c_f, hf_s, ef_s, sf_s, df_s), (c_b, hb_s, eb_s, sb_s, db_s)):
            r0 = pl.multiple_of(c * q, q)
            h = h_s[...]
            ch = _dot(c_ref[0, pl.ds(r0, q), :], h.astype(BF16))
            yacc[pl.ds(r0, q), :] += ch * e_s[pl.ds(r0, q), :].astype(F32)
            h_s[...] = h * d_s[c][0:1, :] + s_s[c]
        return carry

    lax.fori_loop(0, nch, phase_b, 0, unroll=2)

    def phase_c(c, carry):
        r0 = pl.multiple_of(c * q, q)
        zz = z_ref[0, pl.ds(r0, q), :].astype(F32)
        o_ref[0, pl.ds(r0, q), :] = (yacc[pl.ds(r0, q), :] * _silu(zz)).astype(o_ref.dtype)
        return carry

    lax.fori_loop(0, nch, phase_c, 0, unroll=2)


def _ssd_call(xbc_c, bt, z, dt_row, alog_col, dsk, n_lat_chunks, n_ctx_chunks):
    B, T, _ = xbc_c.shape
    G = SSD_GROUPS
    gc = GROUP_COLS
    nch = T // SSD_CHUNK
    c_off = (SSD_WIDTH + SSD_GN) // SSD_STATE
    kern = functools.partial(_ssd_kernel, n_lat_chunks=n_lat_chunks, n_ctx_chunks=n_ctx_chunks)
    return pl.pallas_call(
        kern,
        out_shape=jax.ShapeDtypeStruct((B, T, SSD_WIDTH), BF16),
        grid=(B, G),
        in_specs=[pl.BlockSpec((1, T, gc), lambda b, g: (b, 0, g)),
                  pl.BlockSpec((1, 1, nch, SSD_STATE, SSD_CHUNK), lambda b, g: (b, g, 0, 0, 0)),
                  pl.BlockSpec((1, T, SSD_STATE), lambda b, g: (b, 0, c_off + g)),
                  pl.BlockSpec((1, T, gc), lambda b, g: (b, 0, g)),
                  pl.BlockSpec((1, 1, nch, 2 * SSD_HPG, SSD_CHUNK), lambda b, g: (b, g, 0, 0, 0)),
                  pl.BlockSpec((1, 2 * SSD_HPG, 1), lambda b, g: (g, 0, 0)),
                  pl.BlockSpec((1, 1, gc), lambda b, g: (g, 0, 0))],
        out_specs=pl.BlockSpec((1, T, gc), lambda b, g: (b, 0, g)),
        scratch_shapes=[pltpu.VMEM((T, gc), F32), pltpu.VMEM((T, gc), BF16), pltpu.VMEM((T, gc), BF16),
                        pltpu.VMEM((nch, SSD_STATE, gc), F32), pltpu.VMEM((nch, SSD_STATE, gc), F32),
                        pltpu.VMEM((nch, SUB, gc), F32), pltpu.VMEM((nch, SUB, gc), F32),
                        pltpu.VMEM((SSD_STATE, gc), F32), pltpu.VMEM((SSD_STATE, gc), F32)],
        compiler_params=_cparams(("arbitrary", "arbitrary"), VMEM_BIG),
        name="ssd",
    )(xbc_c, bt, xbc_c, z, dt_row, alog_col, dsk)


def _attn_kernel(lam_ref, q_ref, k_ref, v_ref, gd_ref, o_ref, vext, *, lam_init):
    hw = 2 * DIFF_DIM

    @pl.when(pl.program_id(2) == 0)
    def _():
        col = lax.broadcasted_iota(jnp.int32, (v_ref.shape[1], hw), 1)
        vext[:, :hw] = v_ref[0]
        vext[:, hw:] = jnp.where(col == 0, 1.0, 0.0).astype(BF16)

    lane = lax.broadcasted_iota(jnp.int32, (1, hw), 1)
    qt = q_ref[0]
    kt = k_ref[0]
    zero = jnp.zeros_like(qt)
    scores = [lax.dot_general(jnp.where((lane >= DIFF_DIM) == (m == 1), qt, zero), kt,
                              (((1,), (1,)), ((), ())), preferred_element_type=F32) for m in range(2)]
    heads = []
    for s in scores:
        p = jnp.exp2(s - jnp.max(s, axis=1, keepdims=True)).astype(BF16)
        ov = _dot(p, vext[...])
        heads.append(ov[:, :hw] / ov[:, hw:hw + 1])

    lv = lam_ref[...]
    lam = (jnp.exp(jnp.sum(lv[0:1] * lv[1:2], axis=1, keepdims=True))
           - jnp.exp(jnp.sum(lv[2:3] * lv[3:4], axis=1, keepdims=True)) + lam_init)
    o = heads[0] - lam * heads[1]
    o = o * lax.rsqrt(jnp.mean(o * o, axis=1, keepdims=True) + RMS_EPS) * (1.0 - lam_init)
    o_ref[0] = (o * _silu(gd_ref[0].astype(F32))).astype(o_ref.dtype)


def _attn_call(lam_vecs, q, k, v, gd, tq, q_tile0, n_q_tiles, kv_rows, kv_blk, lam_init):
    B = q.shape[0]
    hw = 2 * DIFF_DIM
    kern = functools.partial(_attn_kernel, lam_init=lam_init)
    return pl.pallas_call(
        kern,
        out_shape=jax.ShapeDtypeStruct((B, n_q_tiles * tq, DIFF_WIDTH), BF16),
        grid=(B, DIFF_HEADS, n_q_tiles),
        in_specs=[pl.BlockSpec((4, DIFF_DIM), lambda b, h, i: (0, 0)),
                  pl.BlockSpec((1, tq, hw), lambda b, h, i: (b, q_tile0 + i, h)),
                  pl.BlockSpec((1, kv_rows, hw), lambda b, h, i: (b, kv_blk, h)),
                  pl.BlockSpec((1, kv_rows, hw), lambda b, h, i: (b, kv_blk, h)),
                  pl.BlockSpec((1, tq, hw), lambda b, h, i: (b, q_tile0 + i, h))],
        out_specs=pl.BlockSpec((1, tq, hw), lambda b, h, i: (b, i, h)),
        scratch_shapes=[pltpu.VMEM((kv_rows, 2 * hw), BF16)],
        compiler_params=_cparams(("arbitrary", "arbitrary", "arbitrary"), VMEM_BIG),
        name="diff_attn",
    )(lam_vecs, q, k, v, gd)


def _hyfilt_kernel(feat_ref, t_ref, w1_ref, b1_ref, w2_ref, b2_ref, fr_ref, w3a_ref, b3a_ref,
                   w3b_ref, b3b_ref, dl_ref, oa_ref, ob_ref):
    fr = fr_ref[...]
    h1 = jnp.sin(fr * (_dot3(feat_ref[...], w1_ref[...]) + b1_ref[...]))
    h2 = jnp.sin(fr * (_dot3(h1, w2_ref[...]) + b2_ref[...]))
    win = jnp.exp(-t_ref[...] * dl_ref[...])
    ca = (_dot3(h2, w3a_ref[...]) + b3a_ref[...]) * win
    cb = (_dot3(h2, w3b_ref[...]) + b3b_ref[...]) * win
    rowid = lax.broadcasted_iota(jnp.int32, (ca.shape[0], 1), 0)
    cb = jnp.where(rowid == 0, 0.0, cb)
    energy = jnp.sum(ca * ca, axis=0, keepdims=True) + jnp.sum(cb * cb, axis=0, keepdims=True)
    sc = lax.rsqrt(energy + RMS_EPS)
    oa_ref[0] = ca * sc
    ob_ref[0] = cb * sc


def _hyfilt_call(feats, tpos, w1p, b1, w2, b2, freq, w3, b3, deltas):
    L = feats.shape[0]
    cb = HY_CB
    ncb = HY_WIDTH // cb
    hid = HY_HIDDEN

    def full(shape):
        return pl.BlockSpec(shape, lambda o, j: (0,) * len(shape))

    def w3spec(d):
        return pl.BlockSpec((hid, cb), lambda o, j: (0, (d * HY_ORDER + o) * ncb + j))

    def b3spec(d):
        return pl.BlockSpec((1, cb), lambda o, j: (0, (d * HY_ORDER + o) * ncb + j))

    out = jax.ShapeDtypeStruct((HY_ORDER, L, HY_WIDTH), F32)
    ospec = pl.BlockSpec((1, L, cb), lambda o, j: (o, 0, j))
    return pl.pallas_call(
        _hyfilt_kernel,
        out_shape=(out, out),
        grid=(HY_ORDER, ncb),
        in_specs=[full((L, hid)), full((L, 1)), full((hid, hid)), full((1, hid)), full((hid, hid)),
                  full((1, hid)), full((1, hid)), w3spec(0), b3spec(0), w3spec(1), b3spec(1),
                  pl.BlockSpec((1, cb), lambda o, j: (0, j))],
        out_specs=(ospec, ospec),
        compiler_params=_cparams(("arbitrary", "arbitrary"), VMEM_BIG),
        name="hy_filter",
    )(feats, tpos, w1p, b1, w2, b2, freq, w3, b3, w3, b3, deltas)


def _twiddle(twc_ref, tws_ref, k1, width):
    reps = width // LANES
    c = jnp.concatenate([twc_ref[k1]] * reps, axis=1)
    s = jnp.concatenate([tws_ref[k1]] * reps, axis=1)
    return c, s


def _hyspec_kernel(fa_ref, fb_ref, gl1_ref, twc_ref, tws_ref, f64_ref, kr_ref, ki_ref, wsr, wsi, *, k1c):
    kc = pl.program_id(2)
    cw = fa_ref.shape[-1]
    gl1 = gl1_ref[0]
    for j in range(DFT2 // SUB):
        xa = fa_ref[0, :, SUB * j:SUB * (j + 1), :]
        xb = fb_ref[0, :, SUB * j:SUB * (j + 1), :]
        h = xa.shape[0]
        xs = jnp.concatenate([xa.reshape(h * SUB, cw), xb.reshape(h * SUB, cw)], axis=1).astype(BF16)
        a = _dot(gl1, xs)
        wsr[:, SUB * j:SUB * (j + 1), :] = a[:k1c * SUB].reshape(k1c, SUB, 2 * cw)
        wsi[:, SUB * j:SUB * (j + 1), :] = a[k1c * SUB:].reshape(k1c, SUB, 2 * cw)
    f64 = f64_ref[...]

    def body(k, carry):
        c, s = _twiddle(twc_ref, tws_ref, kc * k1c + k, 2 * cw)
        ar = wsr[k]
        ai = wsi[k]
        br = ar * c + ai * s
        bi = ai * c - ar * s
        x = _dot(f64, jnp.concatenate([br, bi], axis=0).astype(BF16))
        xr = x[:DFT2]
        xi = x[DFT2:]
        r0 = pl.multiple_of(k * DFT2, DFT2)
        kr_ref[0, pl.ds(r0, DFT2), :] = (xr[:, :cw] + xr[:, cw:]).astype(kr_ref.dtype)
        ki_ref[0, pl.ds(r0, DFT2), :] = (xi[:, :cw] - xi[:, cw:]).astype(ki_ref.dtype)
        return carry

    lax.fori_loop(0, k1c, body, 0, unroll=math.gcd(HY_UNROLL, k1c))


def _hyspec_call(fa, fb, tabs):
    _, L, _ = fa.shape
    H = L // DFT2
    k1c, k1n = tabs["k1c"], tabs["k1n"]
    nkc = k1n // k1c
    cb = HY_CB
    ncb = HY_WIDTH // cb
    fa4 = fa.reshape(HY_ORDER, H, DFT2, HY_WIDTH)
    fb4 = fb.reshape(HY_ORDER, H, DFT2, HY_WIDTH)
    out = jax.ShapeDtypeStruct((HY_ORDER, k1n * DFT2, HY_WIDTH), BF16)
    ospec = pl.BlockSpec((1, k1c * DFT2, cb), lambda o, j, kc: (o, kc, j))
    fspec = pl.BlockSpec((1, H, DFT2, cb), lambda o, j, kc: (o, 0, 0, j))
    kern = functools.partial(_hyspec_kernel, k1c=k1c)
    return pl.pallas_call(
        kern,
        out_shape=(out, out),
        grid=(HY_ORDER, ncb, nkc),
        in_specs=[fspec, fspec,
                  pl.BlockSpec((1, 2 * k1c * SUB, H * SUB), lambda o, j, kc: (kc, 0, 0)),
                  pl.BlockSpec((k1n, DFT2, LANES), lambda o, j, kc: (0, 0, 0), pipeline_mode=pl.Buffered(1)),
                  pl.BlockSpec((k1n, DFT2, LANES), lambda o, j, kc: (0, 0, 0), pipeline_mode=pl.Buffered(1)),
                  pl.BlockSpec((2 * DFT2, 2 * DFT2), lambda o, j, kc: (0, 0))],
        out_specs=(ospec, ospec),
        scratch_shapes=[pltpu.VMEM((k1c, DFT2, 2 * cb), F32), pltpu.VMEM((k1c, DFT2, 2 * cb), F32)],
        compiler_params=_cparams(("arbitrary", "arbitrary", "arbitrary"), VMEM_BIG),
        name="hy_spectrum",
    )(fa4, fb4, tabs["gl1"], tabs["twc"], tabs["tws"], tabs["f64"])


def _hyconv_kernel(*refs, k1c, nkc, n_fft, final):
    u_ref, g_ref = refs[0], refs[1]
    pos = 2
    gh_ref = None
    if final:
        gh_ref = refs[pos]
        pos += 1
    kr_ref, ki_ref, bias_ref, gl1_ref, gl1i_ref, twc_ref, tws_ref, f64_ref, f64i_ref = refs[pos:pos + 9]
    o_ref, wsr, wsi, acc = refs[pos + 9:pos + 13]
    kc = pl.program_id(2)
    cw = u_ref.shape[-1]
    h = u_ref.shape[1]

    gl1 = gl1_ref[0]
    for j in range(DFT2 // SUB):
        xs = u_ref[0, :, SUB * j:SUB * (j + 1), :].reshape(h * SUB, cw).astype(BF16)
        a = _dot(gl1, xs)
        wsr[:, SUB * j:SUB * (j + 1), :] = a[:k1c * SUB].reshape(k1c, SUB, cw)
        wsi[:, SUB * j:SUB * (j + 1), :] = a[k1c * SUB:].reshape(k1c, SUB, cw)
    f64 = f64_ref[...]
    f64i = f64i_ref[...]

    def body(k, carry):
        c, s = _twiddle(twc_ref, tws_ref, kc * k1c + k, cw)
        ar = wsr[k]
        ai = wsi[k]
        br = ar * c + ai * s
        bi = ai * c - ar * s
        x = _dot(f64, jnp.concatenate([br, bi], axis=0).astype(BF16))
        xr = x[:DFT2]
        xi = x[DFT2:]
        r0 = pl.multiple_of(k * DFT2, DFT2)
        kr = kr_ref[0, pl.ds(r0, DFT2), :].astype(F32)
        ki = ki_ref[0, pl.ds(r0, DFT2), :].astype(F32)
        yr = xr * kr - xi * ki
        yi = xr * ki + xi * kr
        cc = _dot(f64i, jnp.concatenate([yr, yi], axis=0).astype(BF16))
        cr = cc[:DFT2]
        ci = cc[DFT2:]
        wsr[k] = cr * c - ci * s
        wsi[k] = ci * c + cr * s
        return carry

    lax.fori_loop(0, k1c, body, 0, unroll=math.gcd(HY_UNROLL, k1c))

    @pl.when(kc == 0)
    def _():
        acc[...] = jnp.zeros_like(acc)

    gl1i = gl1i_ref[0]
    for j in range(DFT2 // SUB):
        dr = wsr[:, SUB * j:SUB * (j + 1), :].reshape(k1c * SUB, cw)
        di = wsi[:, SUB * j:SUB * (j + 1), :].reshape(k1c * SUB, cw)
        yj = _dot(gl1i, jnp.concatenate([dr, di], axis=0).astype(BF16))
        acc[:, SUB * j:SUB * (j + 1), :] += yj.reshape(h, SUB, cw)

    @pl.when(kc == nkc - 1)
    def _():
        u = u_ref[0]
        y = g_ref[0] * (acc[...] * (1.0 / n_fft) + bias_ref[0] * u)
        if final:
            y = y * _silu(gh_ref[0].astype(F32))
        o_ref[0] = y.astype(o_ref.dtype)


def _hyconv_call(u4, u_rb, u_cb, g4, g_rb, g_cb, gh4, gh_rb, kr, ki, bias, order, tabs, H, out_dtype):
    B = u4.shape[0]
    k1c, k1n = tabs["k1c"], tabs["k1n"]
    nkc = k1n // k1c
    cb = HY_CB
    final = gh4 is not None

    def seq_spec(rb, off):
        return pl.BlockSpec((1, H, DFT2, cb), lambda b, j, kc: (b, rb, 0, off + j))

    in_specs = [seq_spec(u_rb, u_cb), seq_spec(g_rb, g_cb)]
    args = [u4, g4]
    if final:
        in_specs.append(seq_spec(gh_rb, 0))
        args.append(gh4)
    in_specs += [
        pl.BlockSpec((1, k1c * DFT2, cb), lambda b, j, kc: (order, kc, j)),
        pl.BlockSpec((1, k1c * DFT2, cb), lambda b, j, kc: (order, kc, j)),
        pl.BlockSpec((1, 1, cb), lambda b, j, kc: (order, 0, j)),
        pl.BlockSpec((1, 2 * k1c * SUB, H * SUB), lambda b, j, kc: (kc, 0, 0)),
        pl.BlockSpec((1, H * SUB, 2 * k1c * SUB), lambda b, j, kc: (kc, 0, 0)),
        pl.BlockSpec((k1n, DFT2, LANES), lambda b, j, kc: (0, 0, 0), pipeline_mode=pl.Buffered(1)),
        pl.BlockSpec((k1n, DFT2, LANES), lambda b, j, kc: (0, 0, 0), pipeline_mode=pl.Buffered(1)),
        pl.BlockSpec((2 * DFT2, 2 * DFT2), lambda b, j, kc: (0, 0)),
        pl.BlockSpec((2 * DFT2, 2 * DFT2), lambda b, j, kc: (0, 0)),
    ]
    args += [kr, ki, bias.reshape(HY_ORDER, 1, HY_WIDTH), tabs["gl1"], tabs["gl1i"], tabs["twc"], tabs["tws"],
             tabs["f64"], tabs["f64i"]]
    kern = functools.partial(_hyconv_kernel, k1c=k1c, nkc=nkc, n_fft=float(2 * H * DFT2), final=final)
    return pl.pallas_call(
        kern,
        out_shape=jax.ShapeDtypeStruct((B, H, DFT2, HY_WIDTH), out_dtype),
        grid=(B, HY_WIDTH // cb, nkc),
        in_specs=in_specs,
        out_specs=pl.BlockSpec((1, H, DFT2, cb), lambda b, j, kc: (b, 0, 0, j)),
        scratch_shapes=[pltpu.VMEM((k1c, DFT2, cb), F32), pltpu.VMEM((k1c, DFT2, cb), F32),
                        pltpu.VMEM((H, DFT2, cb), F32)],
        compiler_params=_cparams(("arbitrary", "arbitrary", "arbitrary"), VMEM_BIG),
        name="hy_conv",
    )(*args)


def _dft_tables(H):
    n1 = 2 * H
    n = n1 * DFT2
    if n1 >= 2 * HY_K1C:
        k1c = HY_K1C
    else:
        k1c = n1 // 2 + 1
    nkc = -(-(n1 // 2 + 1) // k1c)
    k1n = nkc * k1c
    k1 = np.arange(k1n, dtype=np.float64)
    live = (k1 <= n1 // 2).astype(np.float64)
    wgt = np.where((k1 == 0) | (k1 == n1 // 2), 1.0, 2.0) * live
    th = 2.0 * np.pi * np.outer(k1, np.arange(H, dtype=np.float64)) / n1
    eye = np.eye(SUB)
    gre = np.kron(np.cos(th) * live[:, None], eye).reshape(nkc, k1c * SUB, H * SUB)
    gim = np.kron(-np.sin(th) * live[:, None], eye).reshape(nkc, k1c * SUB, H * SUB)
    gl1 = np.concatenate([gre, gim], axis=1)
    gire = np.kron((np.cos(th) * wgt[:, None]).T, eye).reshape(H * SUB, nkc, k1c * SUB).transpose(1, 0, 2)
    giim = np.kron((-np.sin(th) * wgt[:, None]).T, eye).reshape(H * SUB, nkc, k1c * SUB).transpose(1, 0, 2)
    gl1i = np.concatenate([gire, giim], axis=2)
    ph = 2.0 * np.pi * np.outer(k1, np.arange(DFT2, dtype=np.float64)) / n
    twc = np.repeat(np.cos(ph)[:, :, None], LANES, axis=2)
    tws = np.repeat(np.sin(ph)[:, :, None], LANES, axis=2)
    ps = 2.0 * np.pi * np.outer(np.arange(DFT2), np.arange(DFT2)) / DFT2
    c64, s64 = np.cos(ps), np.sin(ps)
    f64 = np.block([[c64, s64], [-s64, c64]])
    f64i = np.block([[c64, -s64], [s64, c64]])
    as_bf = lambda a: jnp.asarray(a, F32).astype(BF16)
    return dict(k1c=k1c, k1n=k1n, gl1=as_bf(gl1), gl1i=as_bf(gl1i), twc=jnp.asarray(twc, F32),
                tws=jnp.asarray(tws, F32), f64=as_bf(f64), f64i=as_bf(f64i))


def _hy_features(L):
    t = np.linspace(0.0, 1.0, L)[:, None]
    wpos = (2.0 * math.pi / L) * np.arange(L, dtype=np.float64)[:, None]
    bands = np.linspace(1e-4, HY_BANDS - 1, HY_BANDS)
    feats = np.concatenate([t, np.cos(wpos * bands), -np.sin(wpos * bands)], -1)
    feats = np.pad(feats, ((0, 0), (0, HY_HIDDEN - HY_EMB)))
    return jnp.asarray(feats, F32), jnp.asarray(t, F32)


def _outproj_kernel(y_ref, od_ref, oh_ref, h_ref, mod_ref, w_ref, nw_ref, g_ref, b_ref, o_ref, *, alpha):
    nb, tm, d = h_ref.shape
    sw = y_ref.shape[-1]
    dw = od_ref.shape[-1]
    ys, ods, ohs = [], [], []
    for j in range(nb):
        y = y_ref[j].astype(F32)
        y = y * lax.rsqrt(jnp.mean(y * y, axis=-1, keepdims=True) + RMS_EPS) * nw_ref[...]
        ys.append(y.astype(BF16))
        ods.append(od_ref[j])
        ohs.append(oh_ref[j])
    cat = (lambda xs: jnp.concatenate(xs, axis=0)) if nb > 1 else (lambda xs: xs[0])
    o = (_dot(cat(ys), w_ref[0:sw, :]) + _dot(cat(ods), w_ref[sw:sw + dw, :])
         + _dot(cat(ohs), w_ref[sw + dw:, :]))
    for j in range(nb):
        gate = mod_ref[j, 0, :, 2 * d:3 * d]
        r = alpha * h_ref[j] + gate * o[j * tm:(j + 1) * tm]
        rc = r - jnp.mean(r, axis=-1, keepdims=True)
        var = jnp.mean(rc * rc, axis=-1, keepdims=True)
        o_ref[j] = rc * lax.rsqrt(var + LN_EPS) * g_ref[...] + b_ref[...]


def _outproj_call(yg, od, oh, hcat, modcat, w_out, norm_w, ln_g, ln_b, n_lat_tiles, n_tiles, alpha):
    B, T, D = hcat.shape
    nb = 2 if B % 2 == 0 else 1

    def kind(i):
        return jnp.where(i >= n_lat_tiles, 1, 0)

    def tok(w):
        return pl.BlockSpec((nb, TM, w), lambda b, i: (b, i, 0))

    def vec(w):
        return pl.BlockSpec((1, w), lambda b, i: (0, 0))

    kern = functools.partial(_outproj_kernel, alpha=alpha)
    return pl.pallas_call(
        kern,
        out_shape=jax.ShapeDtypeStruct((B, n_tiles * TM, D), F32),
        grid=(B // nb, n_tiles),
        in_specs=[tok(SSD_WIDTH), tok(DIFF_WIDTH), tok(HY_WIDTH), tok(D),
                  pl.BlockSpec((nb, 1, 1, 3 * D), lambda b, i: (b, kind(i), 0, 0)),
                  pl.BlockSpec(w_out.shape, lambda b, i: (0, 0)),
                  vec(SSD_WIDTH), vec(D), vec(D)],
        out_specs=tok(D),
        compiler_params=_cparams(("arbitrary", "arbitrary"), VMEM_BIG),
        name="outproj",
    )(yg, od, oh, hcat, modcat, w_out, norm_w.reshape(1, -1), ln_g.reshape(1, -1), ln_b.reshape(1, -1))


def _rope_tables(L, Lc):
    rows = np.repeat(np.arange(L // GRID_W), GRID_W).astype(np.float64)
    cols = np.tile(np.arange(GRID_W), L // GRID_W).astype(np.float64)
    inv = ROPE_BASE ** (-np.arange(ROPE_FREQS, dtype=np.float64) / ROPE_FREQS)
    lane = np.arange(LANES)
    f = lane % ROPE_FREQS
    is_row = (lane % DIFF_DIM) < (DIFF_DIM // 2)
    ang = np.where(is_row[None, :], rows[:, None] * inv[f][None, :], cols[:, None] * inv[f][None, :])
    cos = np.concatenate([np.cos(ang), np.ones((Lc, LANES))], 0)
    sin = np.concatenate([np.sin(ang), np.zeros((Lc, LANES))], 0)
    return jnp.asarray(cos, F32), jnp.asarray(sin, F32)


def _dt_perm():
    g, d, r = np.meshgrid(np.arange(SSD_GROUPS), np.arange(2), np.arange(SSD_HPG), indexing="ij")
    return (d * SSD_HEADS + g * SSD_HPG + r).reshape(-1)


def _layout_w_in(w):
    o = IN_OFFS
    z, xbc, dt = w[:, o[0]:o[1]], w[:, o[1]:o[2]], w[:, o[2]:o[3]]
    q, k, v, gd = w[:, o[3]:o[4]] * (DIFF_SCALE * LOG2E), w[:, o[4]:o[5]], w[:, o[5]:o[6]], w[:, o[6]:o[7]]
    hp, gh = w[:, o[7]:o[8]], w[:, o[8]:o[9]]
    dtp = jnp.pad(dt[:, _dt_perm()], ((0, 0), (0, LANES - SSD_DT)))
    cat = jnp.concatenate([z, xbc, q, k, v, gd, hp, gh, dtp], axis=1)
    return cat.astype(BF16)


def kernel(x, c, ctx, c_ctx, w_ada, b_ada, w_in, ssd_conv_w, ssd_conv_b, ssd_dt_bias, ssd_a_log, ssd_d,
           ssd_norm_w, diff_lambda, hy_conv_w, hy_conv_b, hy_w1, hy_b1, hy_w2, hy_b2, hy_w3, hy_b3,
           hy_freq, hy_bias, w_out, ln_g, ln_b):
    B, L, D = x.shape
    Lc = ctx.shape[1]
    depth = w_in.shape[0]
    T = L + Lc
    assert L % TM == 0 and Lc % TM == 0 and L % GRID_W == 0 and D == D_MODEL
    n_lat_tiles, n_tiles = L // TM, T // TM
    n_lat_chunks, n_ctx_chunks = L // SSD_CHUNK, Lc // SSD_CHUNK
    nch = n_lat_chunks + n_ctx_chunks
    G, R8 = SSD_GROUPS, 2 * SSD_HPG
    alpha = (2 * depth) ** 0.25
    H_lat, H_ctx = L // DFT2, Lc // DFT2
    assert H_lat % H_ctx == 0 and L % Lc == 0

    rope_cos, rope_sin = _rope_tables(L, Lc)
    tabs_lat, tabs_ctx = _dft_tables(H_lat), _dft_tables(H_ctx)
    feats_lat, t_lat = _hy_features(L)
    feats_ctx, t_ctx = _hy_features(Lc)
    deltas = jnp.asarray(np.abs(np.linspace(HY_MIN_DECAY, HY_MAX_DECAY, HY_WIDTH))[None, :], F32)

    rows = -(-(B + 1) // SUB) * SUB
    cc = jnp.zeros((rows, D), F32).at[:B].set(c).at[B].set(c_ctx)
    mod_all = _ada_call(cc, w_ada, b_ada)

    hcat = jnp.concatenate([x, ctx], axis=1)
    perm = _dt_perm()
    for l in range(depth):
        need_ctx = l < depth - 1
        mod = mod_all[l]
        modcat = jnp.stack([mod[:B], jnp.broadcast_to(mod[B], (B, 3 * D))], axis=1)[:, :, None, :]
        dtb = jnp.pad(ssd_dt_bias[l].reshape(-1)[perm], (0, LANES - SSD_DT)).reshape(1, LANES)
        z, xbc, q, k, v, gd, hp, gh, dt = _inproj_call(
            hcat, modcat, _layout_w_in(w_in[l]), dtb, rope_cos, rope_sin, n_lat_tiles)

        xbc_c = _shortconv_call(xbc, ssd_conv_w[l], ssd_conv_b[l], n_lat_chunks, True, BF16)
        dt_row = dt[:, :, :SSD_DT].reshape(B, nch, SSD_CHUNK, G, R8).transpose(0, 3, 1, 4, 2)
        alog = ssd_a_log[l].reshape(2, G, SSD_HPG).transpose(1, 0, 2).reshape(G, R8)
        dsk = jnp.repeat(ssd_d[l].reshape(G, SSD_HPG), SSD_HEAD_DIM, axis=1).reshape(G, 1, GROUP_COLS)
        bt = xbc_c[:, :, SSD_WIDTH:SSD_WIDTH + SSD_GN].reshape(B, nch, SSD_CHUNK, G, SSD_STATE)
        bt = bt.transpose(0, 3, 1, 4, 2)
        yg = _ssd_call(xbc_c, bt, z, dt_row, alog.reshape(G, R8, 1), dsk, n_lat_chunks, n_ctx_chunks)

        lam_init = 0.8 - 0.6 * math.exp(-0.3 * l)
        tq = math.gcd(TQ, L)
        od = _attn_call(diff_lambda[l], q, k, v, gd, tq, 0, L // tq, T, 0, lam_init)
        if need_ctx:
            tqc = math.gcd(tq, Lc)
            od_c = _attn_call(diff_lambda[l], q, k, v, gd, tqc, L // tqc, Lc // tqc, Lc, L // Lc, lam_init)
            od = jnp.concatenate([od, od_c], axis=1)

        hpc = _shortconv_call(hp, hy_conv_w[l], hy_conv_b[l], n_lat_chunks, False, F32)
        hp4 = hpc.reshape(B, T // DFT2, DFT2, 3 * HY_WIDTH)
        gh4 = gh.reshape(B, T // DFT2, DFT2, HY_WIDTH)
        w1p = jnp.pad(hy_w1[l], ((0, HY_HIDDEN - HY_EMB), (0, 0)))
        filt_args = (w1p, hy_b1[l].reshape(1, -1), hy_w2[l], hy_b2[l].reshape(1, -1), hy_freq[l].reshape(1, -1),
                     hy_w3[l], hy_b3[l].reshape(1, -1), deltas)
        ncb = HY_WIDTH // HY_CB
        seqs = [(feats_lat, t_lat, tabs_lat, 0, H_lat)]
        if need_ctx:
            seqs.append((feats_ctx, t_ctx, tabs_ctx, H_lat // H_ctx, H_ctx))
        ohs = []
        for feats, tpos, tabs, rb, H in seqs:
            fa, fb = _hyfilt_call(feats, tpos, *filt_args)
            kr, ki = _hyspec_call(fa, fb, tabs)
            zz = _hyconv_call(hp4, rb, 0, hp4, rb, ncb, None, 0, kr, ki, hy_bias[l], 0, tabs, H, F32)
            oh = _hyconv_call(zz, 0, 0, hp4, rb, 2 * ncb, gh4, rb, kr, ki, hy_bias[l], 1, tabs, H, BF16)
            ohs.append(oh.reshape(B, H * DFT2, HY_WIDTH))
        oh = jnp.concatenate(ohs, axis=1) if need_ctx else ohs[0]

        hcat = _outproj_call(yg, od, oh, hcat, modcat, w_out[l].astype(BF16), ssd_norm_w[l], ln_g[l], ln_b[l],
                             n_lat_tiles, n_tiles if need_ctx else n_lat_tiles, alpha)
    return hcat[:, :L]
```

```python
import functools
import math

import numpy as np
import jax
import jax.numpy as jnp
from jax import lax
from jax.experimental import pallas as pl
from jax.experimental.pallas import tpu as pltpu

F32 = jnp.float32
BF16 = jnp.bfloat16

D_MODEL = 1024
GRID_W = 64
SSD_HEADS = 16
SSD_HEAD_DIM = 64
SSD_WIDTH = SSD_HEADS * SSD_HEAD_DIM
SSD_GROUPS = 4
SSD_HPG = SSD_HEADS // SSD_GROUPS
SSD_STATE = 128
SSD_GN = SSD_GROUPS * SSD_STATE
SSD_XBC = SSD_WIDTH + 2 * SSD_GN
SSD_DT = 2 * SSD_HEADS
SSD_CHUNK = 128
GROUP_COLS = SSD_HPG * SSD_HEAD_DIM
DIFF_HEADS = 4
DIFF_DIM = 64
DIFF_WIDTH = DIFF_HEADS * 2 * DIFF_DIM
DIFF_SCALE = DIFF_DIM ** -0.5
LOG2E = math.log2(math.e)
ROPE_BASE = 10000.0
ROPE_FREQS = DIFF_DIM // 4
HY_WIDTH = 512
HY_ORDER = 2
HY_BANDS = 16
HY_EMB = 1 + 2 * HY_BANDS
HY_HIDDEN = 64
HY_MIN_DECAY = math.log(1e-2) / 1.5
HY_MAX_DECAY = math.log(1e-2) / 0.3
IN_SIZES = (SSD_WIDTH, SSD_XBC, SSD_DT, DIFF_WIDTH, DIFF_WIDTH, DIFF_WIDTH, DIFF_WIDTH, 3 * HY_WIDTH, HY_WIDTH)
IN_OFFS = tuple(int(s) for s in np.cumsum((0,) + IN_SIZES))
LN_EPS = 1e-5
RMS_EPS = 1e-6

LANES = 128
TM = 256
TQ = 512
DFT2 = 64
SUB = 8
HY_CB = 256
HY_UNROLL = 6
HY_K1C = 36
VMEM_BIG = 56 * 2 ** 20
NEG_BIG = -1e30

_W_Z, _W_XBC, _W_Q, _W_K, _W_V, _W_GD, _W_HP, _W_GH, _W_DT, _W_END = (
    0, 1024, 3072, 3584, 4096, 4608, 5120, 6656, 7168, 7296)


def _cparams(sem, vmem=None):
    return pltpu.CompilerParams(dimension_semantics=sem, vmem_limit_bytes=vmem)


def _silu(x):
    h = 0.5 * x
    return h + h * jnp.tanh(h)


def _dot(a, b):
    return jnp.dot(a, b, preferred_element_type=F32)


def _split3(x):
    x1 = x.astype(BF16)
    r1 = x - x1.astype(F32)
    x2 = r1.astype(BF16)
    x3 = (r1 - x2.astype(F32)).astype(BF16)
    return x1, x2, x3


def _dot3(a, b):
    a1 = a.astype(BF16)
    a2 = (a - a1.astype(F32)).astype(BF16)
    b1 = b.astype(BF16)
    b2 = (b - b1.astype(F32)).astype(BF16)
    return _dot(a1, b1) + (_dot(a1, b2) + _dot(a2, b1))


def _ada_kernel(c_ref, w_ref, b_ref, o_ref):
    o_ref[0] = _dot3(_silu(c_ref[...]), w_ref[0]) + b_ref[0]


def _ada_call(cc, w_ada, b_ada):
    depth, d, n3 = w_ada.shape
    rows = cc.shape[0]
    nb = n3 // d
    return pl.pallas_call(
        _ada_kernel,
        out_shape=jax.ShapeDtypeStruct((depth, rows, n3), F32),
        grid=(depth, nb),
        in_specs=[pl.BlockSpec((rows, d), lambda l, j: (0, 0)),
                  pl.BlockSpec((1, d, d), lambda l, j: (l, 0, j)),
                  pl.BlockSpec((1, 1, d), lambda l, j: (l, 0, j))],
        out_specs=pl.BlockSpec((1, rows, d), lambda l, j: (l, 0, j)),
        compiler_params=_cparams(("arbitrary", "arbitrary")),
        name="ada_mod",
    )(cc, w_ada, b_ada.reshape(depth, 1, n3))


def _inproj_kernel(h_ref, mod_ref, w_ref, dtb_ref, cos_ref, sin_ref,
                   z_ref, xbc_ref, q_ref, k_ref, v_ref, gd_ref, hp_ref, gh_ref, dt_ref):
    nb, tm, d = h_ref.shape
    us = []
    for j in range(nb):
        x = h_ref[j]
        xc = x - jnp.mean(x, axis=-1, keepdims=True)
        var = jnp.mean(xc * xc, axis=-1, keepdims=True)
        xn = xc * lax.rsqrt(var + LN_EPS)
        shift = mod_ref[j, 0, :, 0:d]
        scale = mod_ref[j, 0, :, d:2 * d]
        us.append((xn * (1.0 + scale) + shift).astype(BF16))
    u = jnp.concatenate(us, axis=0) if nb > 1 else us[0]

    def mm(c0, c1):
        return _dot(u, w_ref[:, c0:c1])

    def put(ref, val):
        for j in range(nb):
            ref[j] = val[j * tm:(j + 1) * tm].astype(ref.dtype)

    put(z_ref, mm(_W_Z, _W_XBC))
    put(xbc_ref, mm(_W_XBC, _W_Q))
    reps = DIFF_WIDTH // LANES
    cos = jnp.concatenate([cos_ref[...]] * reps, axis=1)
    sin = jnp.concatenate([sin_ref[...]] * reps, axis=1)
    if nb > 1:
        cos = jnp.concatenate([cos] * nb, axis=0)
        sin = jnp.concatenate([sin] * nb, axis=0)
    lane = lax.broadcasted_iota(jnp.int32, (1, DIFF_WIDTH), 1)
    first = (lane % (2 * ROPE_FREQS)) < ROPE_FREQS

    def rope(t):
        partner = jnp.where(first, -pltpu.roll(t, DIFF_WIDTH - ROPE_FREQS, 1), pltpu.roll(t, ROPE_FREQS, 1))
        return t * cos + partner * sin

    put(q_ref, rope(mm(_W_Q, _W_K)))
    put(k_ref, rope(mm(_W_K, _W_V)))
    put(v_ref, mm(_W_V, _W_GD))
    put(gd_ref, mm(_W_GD, _W_HP))
    put(hp_ref, mm(_W_HP, _W_GH))
    put(gh_ref, mm(_W_GH, _W_DT))
    dtx = mm(_W_DT, _W_END) + dtb_ref[...]
    put(dt_ref, jnp.maximum(dtx, 0.0) + jnp.log1p(jnp.exp(-jnp.abs(dtx))))


def _inproj_call(hcat, modcat, w_cat, dtb, rope_cos, rope_sin, n_lat_tiles):
    B, T, D = hcat.shape
    nb = 2 if B % 2 == 0 else 1
    nt = T // TM

    def kind(i):
        return jnp.where(i >= n_lat_tiles, 1, 0)

    widths = (SSD_WIDTH, SSD_XBC, DIFF_WIDTH, DIFF_WIDTH, DIFF_WIDTH, DIFF_WIDTH, 3 * HY_WIDTH, HY_WIDTH, LANES)
    dtypes = (BF16,) * 8 + (F32,)
    out_shape = tuple(jax.ShapeDtypeStruct((B, T, w), dt) for w, dt in zip(widths, dtypes))
    out_specs = tuple(pl.BlockSpec((nb, TM, w), lambda b, i: (b, i, 0)) for w in widths)
    return pl.pallas_call(
        _inproj_kernel,
        out_shape=out_shape,
        grid=(B // nb, nt),
        in_specs=[pl.BlockSpec((nb, TM, D), lambda b, i: (b, i, 0)),
                  pl.BlockSpec((nb, 1, 1, 3 * D), lambda b, i: (b, kind(i), 0, 0)),
                  pl.BlockSpec((D, _W_END), lambda b, i: (0, 0), pipeline_mode=pl.Buffered(1)),
                  pl.BlockSpec((1, LANES), lambda b, i: (0, 0)),
                  pl.BlockSpec((TM, LANES), lambda b, i: (i, 0)),
                  pl.BlockSpec((TM, LANES), lambda b, i: (i, 0))],
        out_specs=out_specs,
        compiler_params=_cparams(("arbitrary", "arbitrary"), VMEM_BIG),
        name="ln_inproj",
    )(hcat, modcat, w_cat, dtb, rope_cos, rope_sin)


def _shortconv_kernel(x_ref, w_ref, b_ref, o_ref, *, n_lat_chunks, n_chunks, act):
    q = SSD_CHUNK
    pack = 16
    pre = 0.5 if act else 1.0
    w0 = w_ref[0:1, :] * pre
    w1 = w_ref[1:2, :] * pre
    w2 = w_ref[2:3, :] * pre
    bias = b_ref[...] * pre
    rowid = lax.broadcasted_iota(jnp.int32, (SUB, 1), 0)

    def body(c, carry):
        r0 = pl.multiple_of(c * q, q)
        cur = x_ref[0, pl.ds(r0, q), :].astype(F32)
        p0 = pl.multiple_of(jnp.maximum(r0 - pack, 0), pack)
        n0 = pl.multiple_of(jnp.minimum(r0 + q, (n_chunks - 1) * q), pack)
        prow = x_ref[0, pl.ds(p0, pack), :].astype(F32)[pack - 1:pack, :]
        nrow = x_ref[0, pl.ds(n0, pack), :].astype(F32)[0:1, :]
        pvalid = jnp.logical_and(c != 0, c != n_lat_chunks)
        nvalid = jnp.logical_and(c != n_lat_chunks - 1, c != n_chunks - 1)
        prow = jnp.where(pvalid, prow, 0.0)
        nrow = jnp.where(nvalid, nrow, 0.0)
        prev = pltpu.roll(cur, 1, 0)
        nxt = pltpu.roll(cur, q - 1, 0)
        prev = jnp.concatenate([jnp.where(rowid == 0, prow, prev[:SUB]), prev[SUB:]], axis=0)
        nxt = jnp.concatenate([nxt[:q - SUB], jnp.where(rowid == SUB - 1, nrow, nxt[q - SUB:])], axis=0)
        y = prev * w0 + cur * w1 + nxt * w2 + bias
        if act:
            y = y + y * jnp.tanh(y)
        o_ref[0, pl.ds(r0, q), :] = y.astype(o_ref.dtype)
        return carry

    lax.fori_loop(0, n_chunks, body, 0)


def _shortconv_call(x, w, b, n_lat_chunks, act, out_dtype):
    B, T, C = x.shape
    cb = 256
    kern = functools.partial(_shortconv_kernel, n_lat_chunks=n_lat_chunks, n_chunks=T // SSD_CHUNK, act=act)
    return pl.pallas_call(
        kern,
        out_shape=jax.ShapeDtypeStruct((B, T, C), out_dtype),
        grid=(B, C // cb),
        in_specs=[pl.BlockSpec((1, T, cb), lambda b, j: (b, 0, j)),
                  pl.BlockSpec((3, cb), lambda b, j: (0, j)),
                  pl.BlockSpec((1, cb), lambda b, j: (0, j))],
        out_specs=pl.BlockSpec((1, T, cb), lambda b, j: (b, 0, j)),
        compiler_params=_cparams(("arbitrary", "arbitrary")),
        name="shortconv_act" if act else "shortconv",
    )(x, w, b.reshape(1, C))


def _ssd_kernel(x_ref, bt_ref, c_ref, z_ref, dtr_ref, alc_ref, dsk_ref, o_ref,
                yacc, ef_s, eb_s, sf_s, sb_s, df_s, db_s, hf_s, hb_s, *, n_lat_chunks, n_ctx_chunks):
    q = SSD_CHUNK
    r_heads = SSD_HPG
    gc = GROUP_COLS
    nch = n_lat_chunks + n_ctx_chunks
    li = lax.broadcasted_iota(jnp.int32, (q, q), 0)
    si = lax.broadcasted_iota(jnp.int32, (q, q), 1)
    low = si <= li
    upp = si >= li
    tri_u = jnp.where(upp, 1.0, 0.0).astype(BF16)
    a_col = -jnp.exp(alc_ref[0]) * LOG2E
    dsk = dsk_ref[0]
    is_fwd = lax.broadcasted_iota(jnp.int32, (2 * r_heads, 1), 0) < r_heads
    lane_head = lax.broadcasted_iota(jnp.int32, (1, gc), 1) // SSD_HEAD_DIM
    first_half = lax.broadcasted_iota(jnp.int32, (1, q), 1) < SSD_HEAD_DIM

    def by_head(parts):
        return jnp.concatenate([jnp.where(first_half, parts[0], parts[1]),
                                jnp.where(first_half, parts[2], parts[3])], axis=1)

    def phase_a(c, carry):
        r0 = pl.multiple_of(c * q, q)
        xh = x_ref[0, pl.ds(r0, q), :]
        cc = c_ref[0, pl.ds(r0, q), :]
        dtr = dtr_ref[0, 0, c]
        dta = dtr * a_col
        d1, d2, d3 = _split3(dta)
        pre = _dot(d1, tri_u) + (_dot(d2, tri_u) + _dot(d3, tri_u))
        tot = pre[:, q - 1:q]
        la = jnp.where(is_fwd, pre, tot - pre + dta)
        wrow = dtr * jnp.exp2(tot - la)
        dec = jnp.exp2(tot)
        bt = bt_ref[0, 0, c]
        cb = _dot(cc, bt)
        zero = jnp.zeros_like(xh)
        xm = [jnp.where(lane_head == r, xh, zero) for r in range(r_heads)]
        y = dsk * xh.astype(F32)
        for pair in range(r_heads // 2):
            efs, ebs = [], []
            for r in (2 * pair, 2 * pair + 1):
                colf = jnp.broadcast_to(la[r:r + 1, :], (q, q)).T
                colb = jnp.broadcast_to(la[r_heads + r:r_heads + r + 1, :], (q, q)).T
                wf = jnp.exp2(jnp.where(low, colf - la[r:r + 1, :], NEG_BIG)) * dtr[r:r + 1, :]
                wb = jnp.exp2(jnp.where(upp, colb - la[r_heads + r:r_heads + r + 1, :], NEG_BIG)) \
                    * dtr[r_heads + r:r_heads + r + 1, :]
                y = y + _dot(((wf + wb) * cb).astype(BF16), xm[r])
                efs.append(jnp.exp2(colf))
                ebs.append(jnp.exp2(colb))
            cols = slice(pair * q, (pair + 1) * q)
            ef_s[pl.ds(r0, q), cols] = jnp.where(first_half, efs[0], efs[1]).astype(ef_s.dtype)
            eb_s[pl.ds(r0, q), cols] = jnp.where(first_half, ebs[0], ebs[1]).astype(eb_s.dtype)
        yacc[pl.ds(r0, q), :] = y
        xst = jnp.concatenate(xm, axis=0)
        btf = bt.astype(F32)
        sts = []
        for d in range(2):
            bw = jnp.concatenate([(btf * wrow[d * r_heads + r:d * r_heads + r + 1, :]).astype(BF16)
                                  for r in range(r_heads)], axis=1)
            sts.append(_dot(bw, xst))
        st = jnp.concatenate(sts, axis=1)
        sf_s[c] = st[:, :gc]
        sb_s[c] = st[:, gc:]
        df_s[c] = jnp.broadcast_to(by_head([dec[r:r + 1, :] for r in range(r_heads)]), (SUB, gc))
        db_s[c] = jnp.broadcast_to(by_head([dec[r_heads + r:r_heads + r + 1, :] for r in range(r_heads)]),
                                   (SUB, gc))
        return carry

    lax.fori_loop(0, nch, phase_a, 0, unroll=2)

    hf_s[...] = jnp.zeros_like(hf_s)
    hb_s[...] = jnp.zeros_like(hb_s)

    def phase_b(j, carry):
        c_f = jnp.where(j < n_ctx_chunks, n_lat_chunks + j, j - n_ctx_chunks)
        c_b = nch - 1 - j
        for c, h_s, e_s, s_s, d_s in ((c_f, hf_s, ef_s, sf_s, df_s), (c_b, hb_s, eb_s, sb_s, db_s)):
            r0 = pl.multiple_of(c * q, q)
            h = h_s[...]
            ch = _dot(c_ref[0, pl.ds(r0, q), :], h.astype(BF16))
            yacc[pl.ds(r0, q), :] += ch * e_s[pl.ds(r0, q), :].astype(F32)
            h_s[...] = h * d_s[c][0:1, :] + s_s[c]
        return carry

    lax.fori_loop(0, nch, phase_b, 0, unroll=4)

    def phase_c(c, carry):
        r0 = pl.multiple_of(c * q, q)
        zz = z_ref[0, pl.ds(r0, q), :].astype(F32)
        o_ref[0, pl.ds(r0, q), :] = (yacc[pl.ds(r0, q), :] * _silu(zz)).astype(o_ref.dtype)
        return carry

    lax.fori_loop(0, nch, phase_c, 0, unroll=2)


def _ssd_call(xbc_c, bt, z, dt_row, alog_col, dsk, n_lat_chunks, n_ctx_chunks):
    B, T, _ = xbc_c.shape
    G = SSD_GROUPS
    gc = GROUP_COLS
    nch = T // SSD_CHUNK
    c_off = (SSD_WIDTH + SSD_GN) // SSD_STATE
    kern = functools.partial(_ssd_kernel, n_lat_chunks=n_lat_chunks, n_ctx_chunks=n_ctx_chunks)
    return pl.pallas_call(
        kern,
        out_shape=jax.ShapeDtypeStruct((B, T, SSD_WIDTH), BF16),
        grid=(B, G),
        in_specs=[pl.BlockSpec((1, T, gc), lambda b, g: (b, 0, g)),
                  pl.BlockSpec((1, 1, nch, SSD_STATE, SSD_CHUNK), lambda b, g: (b, g, 0, 0, 0)),
                  pl.BlockSpec((1, T, SSD_STATE), lambda b, g: (b, 0, c_off + g)),
                  pl.BlockSpec((1, T, gc), lambda b, g: (b, 0, g)),
                  pl.BlockSpec((1, 1, nch, 2 * SSD_HPG, SSD_CHUNK), lambda b, g: (b, g, 0, 0, 0)),
                  pl.BlockSpec((1, 2 * SSD_HPG, 1), lambda b, g: (g, 0, 0)),
                  pl.BlockSpec((1, 1, gc), lambda b, g: (g, 0, 0))],
        out_specs=pl.BlockSpec((1, T, gc), lambda b, g: (b, 0, g)),
        scratch_shapes=[pltpu.VMEM((T, gc), F32), pltpu.VMEM((T, gc), BF16), pltpu.VMEM((T, gc), BF16),
                        pltpu.VMEM((nch, SSD_STATE, gc), F32), pltpu.VMEM((nch, SSD_STATE, gc), F32),
                        pltpu.VMEM((nch, SUB, gc), F32), pltpu.VMEM((nch, SUB, gc), F32),
                        pltpu.VMEM((SSD_STATE, gc), F32), pltpu.VMEM((SSD_STATE, gc), F32)],
        compiler_params=_cparams(("arbitrary", "arbitrary"), VMEM_BIG),
        name="ssd",
    )(xbc_c, bt, xbc_c, z, dt_row, alog_col, dsk)


def _attn_kernel(lam_ref, q_ref, k_ref, v_ref, gd_ref, o_ref, vext, *, lam_init):
    hw = 2 * DIFF_DIM

    @pl.when(pl.program_id(2) == 0)
    def _():
        col = lax.broadcasted_iota(jnp.int32, (v_ref.shape[1], hw), 1)
        vext[:, :hw] = v_ref[0]
        vext[:, hw:] = jnp.where(col == 0, 1.0, 0.0).astype(BF16)

    lane = lax.broadcasted_iota(jnp.int32, (1, hw), 1)
    qt = q_ref[0]
    kt = k_ref[0]
    zero = jnp.zeros_like(qt)
    scores = [lax.dot_general(jnp.where((lane >= DIFF_DIM) == (m == 1), qt, zero), kt,
                              (((1,), (1,)), ((), ())), preferred_element_type=F32) for m in range(2)]
    heads = []
    for s in scores:
        p = jnp.exp2(s - jnp.max(s, axis=1, keepdims=True)).astype(BF16)
        ov = _dot(p, vext[...])
        heads.append(ov[:, :hw] / ov[:, hw:hw + 1])

    lv = lam_ref[...]
    lam = (jnp.exp(jnp.sum(lv[0:1] * lv[1:2], axis=1, keepdims=True))
           - jnp.exp(jnp.sum(lv[2:3] * lv[3:4], axis=1, keepdims=True)) + lam_init)
    o = heads[0] - lam * heads[1]
    o = o * lax.rsqrt(jnp.mean(o * o, axis=1, keepdims=True) + RMS_EPS) * (1.0 - lam_init)
    o_ref[0] = (o * _silu(gd_ref[0].astype(F32))).astype(o_ref.dtype)


def _attn_call(lam_vecs, q, k, v, gd, tq, q_tile0, n_q_tiles, kv_rows, kv_blk, lam_init):
    B = q.shape[0]
    hw = 2 * DIFF_DIM
    kern = functools.partial(_attn_kernel, lam_init=lam_init)
    return pl.pallas_call(
        kern,
        out_shape=jax.ShapeDtypeStruct((B, n_q_tiles * tq, DIFF_WIDTH), BF16),
        grid=(B, DIFF_HEADS, n_q_tiles),
        in_specs=[pl.BlockSpec((4, DIFF_DIM), lambda b, h, i: (0, 0)),
                  pl.BlockSpec((1, tq, hw), lambda b, h, i: (b, q_tile0 + i, h)),
                  pl.BlockSpec((1, kv_rows, hw), lambda b, h, i: (b, kv_blk, h)),
                  pl.BlockSpec((1, kv_rows, hw), lambda b, h, i: (b, kv_blk, h)),
                  pl.BlockSpec((1, tq, hw), lambda b, h, i: (b, q_tile0 + i, h))],
        out_specs=pl.BlockSpec((1, tq, hw), lambda b, h, i: (b, i, h)),
        scratch_shapes=[pltpu.VMEM((kv_rows, 2 * hw), BF16)],
        compiler_params=_cparams(("arbitrary", "arbitrary", "arbitrary"), VMEM_BIG),
        name="diff_attn",
    )(lam_vecs, q, k, v, gd)


def _hyfilt_kernel(feat_ref, t_ref, w1_ref, b1_ref, w2_ref, b2_ref, fr_ref, w3a_ref, b3a_ref,
                   w3b_ref, b3b_ref, dl_ref, oa_ref, ob_ref):
    fr = fr_ref[...]
    h1 = jnp.sin(fr * (_dot3(feat_ref[...], w1_ref[...]) + b1_ref[...]))
    h2 = jnp.sin(fr * (_dot3(h1, w2_ref[...]) + b2_ref[...]))
    win = jnp.exp(-t_ref[...] * dl_ref[...])
    ca = (_dot3(h2, w3a_ref[...]) + b3a_ref[...]) * win
    cb = (_dot3(h2, w3b_ref[...]) + b3b_ref[...]) * win
    rowid = lax.broadcasted_iota(jnp.int32, (ca.shape[0], 1), 0)
    cb = jnp.where(rowid == 0, 0.0, cb)
    energy = jnp.sum(ca * ca, axis=0, keepdims=True) + jnp.sum(cb * cb, axis=0, keepdims=True)
    sc = lax.rsqrt(energy + RMS_EPS)
    oa_ref[0] = ca * sc
    ob_ref[0] = cb * sc


def _hyfilt_call(feats, tpos, w1p, b1, w2, b2, freq, w3, b3, deltas):
    L = feats.shape[0]
    cb = HY_CB
    ncb = HY_WIDTH // cb
    hid = HY_HIDDEN

    def full(shape):
        return pl.BlockSpec(shape, lambda o, j: (0,) * len(shape))

    def w3spec(d):
        return pl.BlockSpec((hid, cb), lambda o, j: (0, (d * HY_ORDER + o) * ncb + j))

    def b3spec(d):
        return pl.BlockSpec((1, cb), lambda o, j: (0, (d * HY_ORDER + o) * ncb + j))

    out = jax.ShapeDtypeStruct((HY_ORDER, L, HY_WIDTH), F32)
    ospec = pl.BlockSpec((1, L, cb), lambda o, j: (o, 0, j))
    return pl.pallas_call(
        _hyfilt_kernel,
        out_shape=(out, out),
        grid=(HY_ORDER, ncb),
        in_specs=[full((L, hid)), full((L, 1)), full((hid, hid)), full((1, hid)), full((hid, hid)),
                  full((1, hid)), full((1, hid)), w3spec(0), b3spec(0), w3spec(1), b3spec(1),
                  pl.BlockSpec((1, cb), lambda o, j: (0, j))],
        out_specs=(ospec, ospec),
        compiler_params=_cparams(("arbitrary", "arbitrary"), VMEM_BIG),
        name="hy_filter",
    )(feats, tpos, w1p, b1, w2, b2, freq, w3, b3, w3, b3, deltas)


def _twiddle(twc_ref, tws_ref, k1, width):
    reps = width // LANES
    c = jnp.concatenate([twc_ref[k1]] * reps, axis=1)
    s = jnp.concatenate([tws_ref[k1]] * reps, axis=1)
    return c, s


def _hyspec_kernel(fa_ref, fb_ref, gl1_ref, twc_ref, tws_ref, f64_ref, kr_ref, ki_ref, wsr, wsi, *, k1c):
    kc = pl.program_id(2)
    cw = fa_ref.shape[-1]
    gl1 = gl1_ref[0]
    for j in range(DFT2 // SUB):
        xa = fa_ref[0, :, SUB * j:SUB * (j + 1), :]
        xb = fb_ref[0, :, SUB * j:SUB * (j + 1), :]
        h = xa.shape[0]
        xs = jnp.concatenate([xa.reshape(h * SUB, cw), xb.reshape(h * SUB, cw)], axis=1).astype(BF16)
        a = _dot(gl1, xs)
        wsr[:, SUB * j:SUB * (j + 1), :] = a[:k1c * SUB].reshape(k1c, SUB, 2 * cw)
        wsi[:, SUB * j:SUB * (j + 1), :] = a[k1c * SUB:].reshape(k1c, SUB, 2 * cw)
    f64 = f64_ref[...]

    def body(k, carry):
        c, s = _twiddle(twc_ref, tws_ref, kc * k1c + k, 2 * cw)
        ar = wsr[k]
        ai = wsi[k]
        br = ar * c + ai * s
        bi = ai * c - ar * s
        x = _dot(f64, jnp.concatenate([br, bi], axis=0).astype(BF16))
        xr = x[:DFT2]
        xi = x[DFT2:]
        r0 = pl.multiple_of(k * DFT2, DFT2)
        kr_ref[0, pl.ds(r0, DFT2), :] = (xr[:, :cw] + xr[:, cw:]).astype(kr_ref.dtype)
        ki_ref[0, pl.ds(r0, DFT2), :] = (xi[:, :cw] - xi[:, cw:]).astype(ki_ref.dtype)
        return carry

    lax.fori_loop(0, k1c, body, 0, unroll=math.gcd(HY_UNROLL, k1c))


def _hyspec_call(fa, fb, tabs):
    _, L, _ = fa.shape
    H = L // DFT2
    k1c, k1n = tabs["k1c"], tabs["k1n"]
    nkc = k1n // k1c
    cb = HY_CB
    ncb = HY_WIDTH // cb
    fa4 = fa.reshape(HY_ORDER, H, DFT2, HY_WIDTH)
    fb4 = fb.reshape(HY_ORDER, H, DFT2, HY_WIDTH)
    out = jax.ShapeDtypeStruct((HY_ORDER, k1n * DFT2, HY_WIDTH), BF16)
    ospec = pl.BlockSpec((1, k1c * DFT2, cb), lambda o, j, kc: (o, kc, j))
    fspec = pl.BlockSpec((1, H, DFT2, cb), lambda o, j, kc: (o, 0, 0, j))
    kern = functools.partial(_hyspec_kernel, k1c=k1c)
    return pl.pallas_call(
        kern,
        out_shape=(out, out),
        grid=(HY_ORDER, ncb, nkc),
        in_specs=[fspec, fspec,
                  pl.BlockSpec((1, 2 * k1c * SUB, H * SUB), lambda o, j, kc: (kc, 0, 0)),
                  pl.BlockSpec((k1n, DFT2, LANES), lambda o, j, kc: (0, 0, 0), pipeline_mode=pl.Buffered(1)),
                  pl.BlockSpec((k1n, DFT2, LANES), lambda o, j, kc: (0, 0, 0), pipeline_mode=pl.Buffered(1)),
                  pl.BlockSpec((2 * DFT2, 2 * DFT2), lambda o, j, kc: (0, 0))],
        out_specs=(ospec, ospec),
        scratch_shapes=[pltpu.VMEM((k1c, DFT2, 2 * cb), F32), pltpu.VMEM((k1c, DFT2, 2 * cb), F32)],
        compiler_params=_cparams(("arbitrary", "arbitrary", "arbitrary"), VMEM_BIG),
        name="hy_spectrum",
    )(fa4, fb4, tabs["gl1"], tabs["twc"], tabs["tws"], tabs["f64"])


def _hyconv_kernel(*refs, k1c, nkc, n_fft, final):
    u_ref, g_ref = refs[0], refs[1]
    pos = 2
    gh_ref = None
    if final:
        gh_ref = refs[pos]
        pos += 1
    kr_ref, ki_ref, bias_ref, gl1_ref, gl1i_ref, twc_ref, tws_ref, f64_ref, f64i_ref = refs[pos:pos + 9]
    o_ref, wsr, wsi, acc = refs[pos + 9:pos + 13]
    kc = pl.program_id(2)
    cw = u_ref.shape[-1]
    h = u_ref.shape[1]

    gl1 = gl1_ref[0]
    for j in range(DFT2 // SUB):
        xs = u_ref[0, :, SUB * j:SUB * (j + 1), :].reshape(h * SUB, cw).astype(BF16)
        a = _dot(gl1, xs)
        wsr[:, SUB * j:SUB * (j + 1), :] = a[:k1c * SUB].reshape(k1c, SUB, cw)
        wsi[:, SUB * j:SUB * (j + 1), :] = a[k1c * SUB:].reshape(k1c, SUB, cw)
    f64 = f64_ref[...]
    f64i = f64i_ref[...]

    def body(k, carry):
        c, s = _twiddle(twc_ref, tws_ref, kc * k1c + k, cw)
        ar = wsr[k]
        ai = wsi[k]
        br = ar * c + ai * s
        bi = ai * c - ar * s
        x = _dot(f64, jnp.concatenate([br, bi], axis=0).astype(BF16))
        xr = x[:DFT2]
        xi = x[DFT2:]
        r0 = pl.multiple_of(k * DFT2, DFT2)
        kr = kr_ref[0, pl.ds(r0, DFT2), :].astype(F32)
        ki = ki_ref[0, pl.ds(r0, DFT2), :].astype(F32)
        yr = xr * kr - xi * ki
        yi = xr * ki + xi * kr
        cc = _dot(f64i, jnp.concatenate([yr, yi], axis=0).astype(BF16))
        cr = cc[:DFT2]
        ci = cc[DFT2:]
        wsr[k] = cr * c - ci * s
        wsi[k] = ci * c + cr * s
        return carry

    lax.fori_loop(0, k1c, body, 0, unroll=math.gcd(HY_UNROLL, k1c))

    @pl.when(kc == 0)
    def _():
        acc[...] = jnp.zeros_like(acc)

    gl1i = gl1i_ref[0]
    for j in range(DFT2 // SUB):
        dr = wsr[:, SUB * j:SUB * (j + 1), :].reshape(k1c * SUB, cw)
        di = wsi[:, SUB * j:SUB * (j + 1), :].reshape(k1c * SUB, cw)
        yj = _dot(gl1i, jnp.concatenate([dr, di], axis=0).astype(BF16))
        acc[:, SUB * j:SUB * (j + 1), :] += yj.reshape(h, SUB, cw)

    @pl.when(kc == nkc - 1)
    def _():
        u = u_ref[0]
        y = g_ref[0] * (acc[...] * (1.0 / n_fft) + bias_ref[0] * u)
        if final:
            y = y * _silu(gh_ref[0].astype(F32))
        o_ref[0] = y.astype(o_ref.dtype)


def _hyconv_call(u4, u_rb, u_cb, g4, g_rb, g_cb, gh4, gh_rb, kr, ki, bias, order, tabs, H, out_dtype):
    B = u4.shape[0]
    k1c, k1n = tabs["k1c"], tabs["k1n"]
    nkc = k1n // k1c
    cb = HY_CB
    final = gh4 is not None

    def seq_spec(rb, off):
        return pl.BlockSpec((1, H, DFT2, cb), lambda b, j, kc: (b, rb, 0, off + j))

    in_specs = [seq_spec(u_rb, u_cb), seq_spec(g_rb, g_cb)]
    args = [u4, g4]
    if final:
        in_specs.append(seq_spec(gh_rb, 0))
        args.append(gh4)
    in_specs += [
        pl.BlockSpec((1, k1c * DFT2, cb), lambda b, j, kc: (order, kc, j)),
        pl.BlockSpec((1, k1c * DFT2, cb), lambda b, j, kc: (order, kc, j)),
        pl.BlockSpec((1, 1, cb), lambda b, j, kc: (order, 0, j)),
        pl.BlockSpec((1, 2 * k1c * SUB, H * SUB), lambda b, j, kc: (kc, 0, 0)),
        pl.BlockSpec((1, H * SUB, 2 * k1c * SUB), lambda b, j, kc: (kc, 0, 0)),
        pl.BlockSpec((k1n, DFT2, LANES), lambda b, j, kc: (0, 0, 0), pipeline_mode=pl.Buffered(1)),
        pl.BlockSpec((k1n, DFT2, LANES), lambda b, j, kc: (0, 0, 0), pipeline_mode=pl.Buffered(1)),
        pl.BlockSpec((2 * DFT2, 2 * DFT2), lambda b, j, kc: (0, 0)),
        pl.BlockSpec((2 * DFT2, 2 * DFT2), lambda b, j, kc: (0, 0)),
    ]
    args += [kr, ki, bias.reshape(HY_ORDER, 1, HY_WIDTH), tabs["gl1"], tabs["gl1i"], tabs["twc"], tabs["tws"],
             tabs["f64"], tabs["f64i"]]
    kern = functools.partial(_hyconv_kernel, k1c=k1c, nkc=nkc, n_fft=float(2 * H * DFT2), final=final)
    return pl.pallas_call(
        kern,
        out_shape=jax.ShapeDtypeStruct((B, H, DFT2, HY_WIDTH), out_dtype),
        grid=(B, HY_WIDTH // cb, nkc),
        in_specs=in_specs,
        out_specs=pl.BlockSpec((1, H, DFT2, cb), lambda b, j, kc: (b, 0, 0, j)),
        scratch_shapes=[pltpu.VMEM((k1c, DFT2, cb), F32), pltpu.VMEM((k1c, DFT2, cb), F32),
                        pltpu.VMEM((H, DFT2, cb), F32)],
        compiler_params=_cparams(("arbitrary", "arbitrary", "arbitrary"), VMEM_BIG),
        name="hy_conv",
    )(*args)


def _dft_tables(H):
    n1 = 2 * H
    n = n1 * DFT2
    if n1 >= 2 * HY_K1C:
        k1c = HY_K1C
    else:
        k1c = n1 // 2 + 1
    nkc = -(-(n1 // 2 + 1) // k1c)
    k1n = nkc * k1c
    k1 = np.arange(k1n, dtype=np.float64)
    live = (k1 <= n1 // 2).astype(np.float64)
    wgt = np.where((k1 == 0) | (k1 == n1 // 2), 1.0, 2.0) * live
    th = 2.0 * np.pi * np.outer(k1, np.arange(H, dtype=np.float64)) / n1
    eye = np.eye(SUB)
    gre = np.kron(np.cos(th) * live[:, None], eye).reshape(nkc, k1c * SUB, H * SUB)
    gim = np.kron(-np.sin(th) * live[:, None], eye).reshape(nkc, k1c * SUB, H * SUB)
    gl1 = np.concatenate([gre, gim], axis=1)
    gire = np.kron((np.cos(th) * wgt[:, None]).T, eye).reshape(H * SUB, nkc, k1c * SUB).transpose(1, 0, 2)
    giim = np.kron((-np.sin(th) * wgt[:, None]).T, eye).reshape(H * SUB, nkc, k1c * SUB).transpose(1, 0, 2)
    gl1i = np.concatenate([gire, giim], axis=2)
    ph = 2.0 * np.pi * np.outer(k1, np.arange(DFT2, dtype=np.float64)) / n
    twc = np.repeat(np.cos(ph)[:, :, None], LANES, axis=2)
    tws = np.repeat(np.sin(ph)[:, :, None], LANES, axis=2)
    ps = 2.0 * np.pi * np.outer(np.arange(DFT2), np.arange(DFT2)) / DFT2
    c64, s64 = np.cos(ps), np.sin(ps)
    f64 = np.block([[c64, s64], [-s64, c64]])
    f64i = np.block([[c64, -s64], [s64, c64]])
    as_bf = lambda a: jnp.asarray(a, F32).astype(BF16)
    return dict(k1c=k1c, k1n=k1n, gl1=as_bf(gl1), gl1i=as_bf(gl1i), twc=jnp.asarray(twc, F32),
                tws=jnp.asarray(tws, F32), f64=as_bf(f64), f64i=as_bf(f64i))


def _hy_features(L):
    t = np.linspace(0.0, 1.0, L)[:, None]
    wpos = (2.0 * math.pi / L) * np.arange(L, dtype=np.float64)[:, None]
    bands = np.linspace(1e-4, HY_BANDS - 1, HY_BANDS)
    feats = np.concatenate([t, np.cos(wpos * bands), -np.sin(wpos * bands)], -1)
    feats = np.pad(feats, ((0, 0), (0, HY_HIDDEN - HY_EMB)))
    return jnp.asarray(feats, F32), jnp.asarray(t, F32)


def _outproj_kernel(y_ref, od_ref, oh_ref, h_ref, mod_ref, w_ref, nw_ref, g_ref, b_ref, o_ref, *, alpha):
    nb, tm, d = h_ref.shape
    sw = y_ref.shape[-1]
    dw = od_ref.shape[-1]
    ys, ods, ohs = [], [], []
    for j in range(nb):
        y = y_ref[j].astype(F32)
        y = y * lax.rsqrt(jnp.mean(y * y, axis=-1, keepdims=True) + RMS_EPS) * nw_ref[...]
        ys.append(y.astype(BF16))
        ods.append(od_ref[j])
        ohs.append(oh_ref[j])
    cat = (lambda xs: jnp.concatenate(xs, axis=0)) if nb > 1 else (lambda xs: xs[0])
    o = (_dot(cat(ys), w_ref[0:sw, :]) + _dot(cat(ods), w_ref[sw:sw + dw, :])
         + _dot(cat(ohs), w_ref[sw + dw:, :]))
    for j in range(nb):
        gate = mod_ref[j, 0, :, 2 * d:3 * d]
        r = alpha * h_ref[j] + gate * o[j * tm:(j + 1) * tm]
        rc = r - jnp.mean(r, axis=-1, keepdims=True)
        var = jnp.mean(rc * rc, axis=-1, keepdims=True)
        o_ref[j] = rc * lax.rsqrt(var + LN_EPS) * g_ref[...] + b_ref[...]


def _outproj_call(yg, od, oh, hcat, modcat, w_out, norm_w, ln_g, ln_b, n_lat_tiles, n_tiles, alpha):
    B, T, D = hcat.shape
    nb = 2 if B % 2 == 0 else 1

    def kind(i):
        return jnp.where(i >= n_lat_tiles, 1, 0)

    def tok(w):
        return pl.BlockSpec((nb, TM, w), lambda b, i: (b, i, 0))

    def vec(w):
        return pl.BlockSpec((1, w), lambda b, i: (0, 0))

    kern = functools.partial(_outproj_kernel, alpha=alpha)
    return pl.pallas_call(
        kern,
        out_shape=jax.ShapeDtypeStruct((B, n_tiles * TM, D), F32),
        grid=(B // nb, n_tiles),
        in_specs=[tok(SSD_WIDTH), tok(DIFF_WIDTH), tok(HY_WIDTH), tok(D),
                  pl.BlockSpec((nb, 1, 1, 3 * D), lambda b, i: (b, kind(i), 0, 0)),
                  pl.BlockSpec(w_out.shape, lambda b, i: (0, 0)),
                  vec(SSD_WIDTH), vec(D), vec(D)],
        out_specs=tok(D),
        compiler_params=_cparams(("arbitrary", "arbitrary"), VMEM_BIG),
        name="outproj",
    )(yg, od, oh, hcat, modcat, w_out, norm_w.reshape(1, -1), ln_g.reshape(1, -1), ln_b.reshape(1, -1))


def _rope_tables(L, Lc):
    rows = np.repeat(np.arange(L // GRID_W), GRID_W).astype(np.float64)
    cols = np.tile(np.arange(GRID_W), L // GRID_W).astype(np.float64)
    inv = ROPE_BASE ** (-np.arange(ROPE_FREQS, dtype=np.float64) / ROPE_FREQS)
    lane = np.arange(LANES)
    f = lane % ROPE_FREQS
    is_row = (lane % DIFF_DIM) < (DIFF_DIM // 2)
    ang = np.where(is_row[None, :], rows[:, None] * inv[f][None, :], cols[:, None] * inv[f][None, :])
    cos = np.concatenate([np.cos(ang), np.ones((Lc, LANES))], 0)
    sin = np.concatenate([np.sin(ang), np.zeros((Lc, LANES))], 0)
    return jnp.asarray(cos, F32), jnp.asarray(sin, F32)


def _dt_perm():
    g, d, r = np.meshgrid(np.arange(SSD_GROUPS), np.arange(2), np.arange(SSD_HPG), indexing="ij")
    return (d * SSD_HEADS + g * SSD_HPG + r).reshape(-1)


def _layout_w_in(w):
    o = IN_OFFS
    z, xbc, dt = w[:, o[0]:o[1]], w[:, o[1]:o[2]], w[:, o[2]:o[3]]
    q, k, v, gd = w[:, o[3]:o[4]] * (DIFF_SCALE * LOG2E), w[:, o[4]:o[5]], w[:, o[5]:o[6]], w[:, o[6]:o[7]]
    hp, gh = w[:, o[7]:o[8]], w[:, o[8]:o[9]]
    dtp = jnp.pad(dt[:, _dt_perm()], ((0, 0), (0, LANES - SSD_DT)))
    cat = jnp.concatenate([z, xbc, q, k, v, gd, hp, gh, dtp], axis=1)
    return cat.astype(BF16)


def kernel(x, c, ctx, c_ctx, w_ada, b_ada, w_in, ssd_conv_w, ssd_conv_b, ssd_dt_bias, ssd_a_log, ssd_d,
           ssd_norm_w, diff_lambda, hy_conv_w, hy_conv_b, hy_w1, hy_b1, hy_w2, hy_b2, hy_w3, hy_b3,
           hy_freq, hy_bias, w_out, ln_g, ln_b):
    B, L, D = x.shape
    Lc = ctx.shape[1]
    depth = w_in.shape[0]
    T = L + Lc
    assert L % TM == 0 and Lc % TM == 0 and L % GRID_W == 0 and D == D_MODEL
    n_lat_tiles, n_tiles = L // TM, T // TM
    n_lat_chunks, n_ctx_chunks = L // SSD_CHUNK, Lc // SSD_CHUNK
    nch = n_lat_chunks + n_ctx_chunks
    G, R8 = SSD_GROUPS, 2 * SSD_HPG
    alpha = (2 * depth) ** 0.25
    H_lat, H_ctx = L // DFT2, Lc // DFT2
    assert H_lat % H_ctx == 0 and L % Lc == 0

    rope_cos, rope_sin = _rope_tables(L, Lc)
    tabs_lat, tabs_ctx = _dft_tables(H_lat), _dft_tables(H_ctx)
    feats_lat, t_lat = _hy_features(L)
    feats_ctx, t_ctx = _hy_features(Lc)
    deltas = jnp.asarray(np.abs(np.linspace(HY_MIN_DECAY, HY_MAX_DECAY, HY_WIDTH))[None, :], F32)

    rows = -(-(B + 1) // SUB) * SUB
    cc = jnp.zeros((rows, D), F32).at[:B].set(c).at[B].set(c_ctx)
    mod_all = _ada_call(cc, w_ada, b_ada)

    hcat = jnp.concatenate([x, ctx], axis=1)
    perm = _dt_perm()
    for l in range(depth):
        need_ctx = l < depth - 1
        mod = mod_all[l]
        modcat = jnp.stack([mod[:B], jnp.broadcast_to(mod[B], (B, 3 * D))], axis=1)[:, :, None, :]
        dtb = jnp.pad(ssd_dt_bias[l].reshape(-1)[perm], (0, LANES - SSD_DT)).reshape(1, LANES)
        z, xbc, q, k, v, gd, hp, gh, dt = _inproj_call(
            hcat, modcat, _layout_w_in(w_in[l]), dtb, rope_cos, rope_sin, n_lat_tiles)

        xbc_c = _shortconv_call(xbc, ssd_conv_w[l], ssd_conv_b[l], n_lat_chunks, True, BF16)
        dt_row = dt[:, :, :SSD_DT].reshape(B, nch, SSD_CHUNK, G, R8).transpose(0, 3, 1, 4, 2)
        alog = ssd_a_log[l].reshape(2, G, SSD_HPG).transpose(1, 0, 2).reshape(G, R8)
        dsk = jnp.repeat(ssd_d[l].reshape(G, SSD_HPG), SSD_HEAD_DIM, axis=1).reshape(G, 1, GROUP_COLS)
        bt = xbc_c[:, :, SSD_WIDTH:SSD_WIDTH + SSD_GN].reshape(B, nch, SSD_CHUNK, G, SSD_STATE)
        bt = bt.transpose(0, 3, 1, 4, 2)
        yg = _ssd_call(xbc_c, bt, z, dt_row, alog.reshape(G, R8, 1), dsk, n_lat_chunks, n_ctx_chunks)

        lam_init = 0.8 - 0.6 * math.exp(-0.3 * l)
        tq = math.gcd(TQ, L)
        od = _attn_call(diff_lambda[l], q, k, v, gd, tq, 0, L // tq, T, 0, lam_init)
        if need_ctx:
            tqc = math.gcd(tq, Lc)
            od_c = _attn_call(diff_lambda[l], q, k, v, gd, tqc, L // tqc, Lc // tqc, Lc, L // Lc, lam_init)
            od = jnp.concatenate([od, od_c], axis=1)

        hpc = _shortconv_call(hp, hy_conv_w[l], hy_conv_b[l], n_lat_chunks, False, F32)
        hp4 = hpc.reshape(B, T // DFT2, DFT2, 3 * HY_WIDTH)
        gh4 = gh.reshape(B, T // DFT2, DFT2, HY_WIDTH)
        w1p = jnp.pad(hy_w1[l], ((0, HY_HIDDEN - HY_EMB), (0, 0)))
        filt_args = (w1p, hy_b1[l].reshape(1, -1), hy_w2[l], hy_b2[l].reshape(1, -1), hy_freq[l].reshape(1, -1),
                     hy_w3[l], hy_b3[l].reshape(1, -1), deltas)
        ncb = HY_WIDTH // HY_CB
        seqs = [(feats_lat, t_lat, tabs_lat, 0, H_lat)]
        if need_ctx:
            seqs.append((feats_ctx, t_ctx, tabs_ctx, H_lat // H_ctx, H_ctx))
        ohs = []
        for feats, tpos, tabs, rb, H in seqs:
            fa, fb = _hyfilt_call(feats, tpos, *filt_args)
            kr, ki = _hyspec_call(fa, fb, tabs)
            zz = _hyconv_call(hp4, rb, 0, hp4, rb, ncb, None, 0, kr, ki, hy_bias[l], 0, tabs, H, F32)
            oh = _hyconv_call(zz, 0, 0, hp4, rb, 2 * ncb, gh4, rb, kr, ki, hy_bias[l], 1, tabs, H, BF16)
            ohs.append(oh.reshape(B, H * DFT2, HY_WIDTH))
        oh = jnp.concatenate(ohs, axis=1) if need_ctx else ohs[0]

        hcat = _outproj_call(yg, od, oh, hcat, modcat, w_out[l].astype(BF16), ssd_norm_w[l], ln_g[l], ln_b[l],
                             n_lat_tiles, n_tiles if need_ctx else n_lat_tiles, alpha)
    return hcat[:, :L]
```

```python
import functools
import math

import numpy as np
import jax
import jax.numpy as jnp
from jax import lax
from jax.experimental import pallas as pl
from jax.experimental.pallas import tpu as pltpu

F32 = jnp.float32
BF16 = jnp.bfloat16

D_MODEL = 1024
GRID_W = 64
SSD_HEADS = 16
SSD_HEAD_DIM = 64
SSD_WIDTH = SSD_HEADS * SSD_HEAD_DIM
SSD_GROUPS = 4
SSD_HPG = SSD_HEADS // SSD_GROUPS
SSD_STATE = 128
SSD_GN = SSD_GROUPS * SSD_STATE
SSD_XBC = SSD_WIDTH + 2 * SSD_GN
SSD_DT = 2 * SSD_HEADS
SSD_CHUNK = 128
GROUP_COLS = SSD_HPG * SSD_HEAD_DIM
DIFF_HEADS = 4
DIFF_DIM = 64
DIFF_WIDTH = DIFF_HEADS * 2 * DIFF_DIM
DIFF_SCALE = DIFF_DIM ** -0.5
LOG2E = math.log2(math.e)
ROPE_BASE = 10000.0
ROPE_FREQS = DIFF_DIM // 4
HY_WIDTH = 512
HY_ORDER = 2
HY_BANDS = 16
HY_EMB = 1 + 2 * HY_BANDS
HY_HIDDEN = 64
HY_MIN_DECAY = math.log(1e-2) / 1.5
HY_MAX_DECAY = math.log(1e-2) / 0.3
IN_SIZES = (SSD_WIDTH, SSD_XBC, SSD_DT, DIFF_WIDTH, DIFF_WIDTH, DIFF_WIDTH, DIFF_WIDTH, 3 * HY_WIDTH, HY_WIDTH)
IN_OFFS = tuple(int(s) for s in np.cumsum((0,) + IN_SIZES))
LN_EPS = 1e-5
RMS_EPS = 1e-6

LANES = 128
TM = 256
TQ = 512
DFT2 = 64
SUB = 8
HY_CB = 256
HY_UNROLL = 6
HY_K1C = 36
VMEM_BIG = 56 * 2 ** 20
NEG_BIG = -1e30

_W_Z, _W_XBC, _W_Q, _W_K, _W_V, _W_GD, _W_HP, _W_GH, _W_DT, _W_END = (
    0, 1024, 3072, 3584, 4096, 4608, 5120, 6656, 7168, 7296)


def _cparams(sem, vmem=None):
    return pltpu.CompilerParams(dimension_semantics=sem, vmem_limit_bytes=vmem)


def _silu(x):
    h = 0.5 * x
    return h + h * jnp.tanh(h)


def _dot(a, b):
    return jnp.dot(a, b, preferred_element_type=F32)


def _split3(x):
    x1 = x.astype(BF16)
    r1 = x - x1.astype(F32)
    x2 = r1.astype(BF16)
    x3 = (r1 - x2.astype(F32)).astype(BF16)
    return x1, x2, x3


def _dot3(a, b):
    a1 = a.astype(BF16)
    a2 = (a - a1.astype(F32)).astype(BF16)
    b1 = b.astype(BF16)
    b2 = (b - b1.astype(F32)).astype(BF16)
    return _dot(a1, b1) + (_dot(a1, b2) + _dot(a2, b1))


def _ada_kernel(c_ref, w_ref, b_ref, o_ref):
    o_ref[0] = _dot3(_silu(c_ref[...]), w_ref[0]) + b_ref[0]


def _ada_call(cc, w_ada, b_ada):
    depth, d, n3 = w_ada.shape
    rows = cc.shape[0]
    nb = n3 // d
    return pl.pallas_call(
        _ada_kernel,
        out_shape=jax.ShapeDtypeStruct((depth, rows, n3), F32),
        grid=(depth, nb),
        in_specs=[pl.BlockSpec((rows, d), lambda l, j: (0, 0)),
                  pl.BlockSpec((1, d, d), lambda l, j: (l, 0, j)),
                  pl.BlockSpec((1, 1, d), lambda l, j: (l, 0, j))],
        out_specs=pl.BlockSpec((1, rows, d), lambda l, j: (l, 0, j)),
        compiler_params=_cparams(("arbitrary", "arbitrary")),
        name="ada_mod",
    )(cc, w_ada, b_ada.reshape(depth, 1, n3))


def _inproj_kernel(h_ref, mod_ref, w_ref, dtb_ref, cos_ref, sin_ref,
                   z_ref, xbc_ref, q_ref, k_ref, v_ref, gd_ref, hp_ref, gh_ref, dt_ref):
    nb, tm, d = h_ref.shape
    us = []
    for j in range(nb):
        x = h_ref[j]
        xc = x - jnp.mean(x, axis=-1, keepdims=True)
        var = jnp.mean(xc * xc, axis=-1, keepdims=True)
        xn = xc * lax.rsqrt(var + LN_EPS)
        shift = mod_ref[j, 0, :, 0:d]
        scale = mod_ref[j, 0, :, d:2 * d]
        us.append((xn * (1.0 + scale) + shift).astype(BF16))
    u = jnp.concatenate(us, axis=0) if nb > 1 else us[0]

    def mm(c0, c1):
        return _dot(u, w_ref[:, c0:c1])

    def put(ref, val):
        for j in range(nb):
            ref[j] = val[j * tm:(j + 1) * tm].astype(ref.dtype)

    put(z_ref, mm(_W_Z, _W_XBC))
    put(xbc_ref, mm(_W_XBC, _W_Q))
    reps = DIFF_WIDTH // LANES
    cos = jnp.concatenate([cos_ref[...]] * reps, axis=1)
    sin = jnp.concatenate([sin_ref[...]] * reps, axis=1)
    if nb > 1:
        cos = jnp.concatenate([cos] * nb, axis=0)
        sin = jnp.concatenate([sin] * nb, axis=0)
    lane = lax.broadcasted_iota(jnp.int32, (1, DIFF_WIDTH), 1)
    first = (lane % (2 * ROPE_FREQS)) < ROPE_FREQS

    def rope(t):
        partner = jnp.where(first, -pltpu.roll(t, DIFF_WIDTH - ROPE_FREQS, 1), pltpu.roll(t, ROPE_FREQS, 1))
        return t * cos + partner * sin

    put(q_ref, rope(mm(_W_Q, _W_K)))
    put(k_ref, rope(mm(_W_K, _W_V)))
    put(v_ref, mm(_W_V, _W_GD))
    put(gd_ref, mm(_W_GD, _W_HP))
    put(hp_ref, mm(_W_HP, _W_GH))
    put(gh_ref, mm(_W_GH, _W_DT))
    dtx = mm(_W_DT, _W_END) + dtb_ref[...]
    put(dt_ref, jnp.maximum(dtx, 0.0) + jnp.log1p(jnp.exp(-jnp.abs(dtx))))


def _inproj_call(hcat, modcat, w_cat, dtb, rope_cos, rope_sin, n_lat_tiles):
    B, T, D = hcat.shape
    nb = 2 if B % 2 == 0 else 1
    nt = T // TM

    def kind(i):
        return jnp.where(i >= n_lat_tiles, 1, 0)

    widths = (SSD_WIDTH, SSD_XBC, DIFF_WIDTH, DIFF_WIDTH, DIFF_WIDTH, DIFF_WIDTH, 3 * HY_WIDTH, HY_WIDTH, LANES)
    dtypes = (BF16,) * 8 + (F32,)
    out_shape = tuple(jax.ShapeDtypeStruct((B, T, w), dt) for w, dt in zip(widths, dtypes))
    out_specs = tuple(pl.BlockSpec((nb, TM, w), lambda b, i: (b, i, 0)) for w in widths)
    return pl.pallas_call(
        _inproj_kernel,
        out_shape=out_shape,
        grid=(B // nb, nt),
        in_specs=[pl.BlockSpec((nb, TM, D), lambda b, i: (b, i, 0)),
                  pl.BlockSpec((nb, 1, 1, 3 * D), lambda b, i: (b, kind(i), 0, 0)),
                  pl.BlockSpec((D, _W_END), lambda b, i: (0, 0), pipeline_mode=pl.Buffered(1)),
                  pl.BlockSpec((1, LANES), lambda b, i: (0, 0)),
                  pl.BlockSpec((TM, LANES), lambda b, i: (i, 0)),
                  pl.BlockSpec((TM, LANES), lambda b, i: (i, 0))],
        out_specs=out_specs,
        compiler_params=_cparams(("arbitrary", "arbitrary"), VMEM_BIG),
        name="ln_inproj",
    )(hcat, modcat, w_cat, dtb, rope_cos, rope_sin)


def _shortconv_kernel(x_ref, w_ref, b_ref, o_ref, *, n_lat_chunks, n_chunks, act):
    q = SSD_CHUNK
    pack = 16
    pre = 0.5 if act else 1.0
    w0 = w_ref[0:1, :] * pre
    w1 = w_ref[1:2, :] * pre
    w2 = w_ref[2:3, :] * pre
    bias = b_ref[...] * pre
    rowid = lax.broadcasted_iota(jnp.int32, (SUB, 1), 0)

    def body(c, carry):
        r0 = pl.multiple_of(c * q, q)
        cur = x_ref[0, pl.ds(r0, q), :].astype(F32)
        p0 = pl.multiple_of(jnp.maximum(r0 - pack, 0), pack)
        n0 = pl.multiple_of(jnp.minimum(r0 + q, (n_chunks - 1) * q), pack)
        prow = x_ref[0, pl.ds(p0, pack), :].astype(F32)[pack - 1:pack, :]
        nrow = x_ref[0, pl.ds(n0, pack), :].astype(F32)[0:1, :]
        pvalid = jnp.logical_and(c != 0, c != n_lat_chunks)
        nvalid = jnp.logical_and(c != n_lat_chunks - 1, c != n_chunks - 1)
        prow = jnp.where(pvalid, prow, 0.0)
        nrow = jnp.where(nvalid, nrow, 0.0)
        prev = pltpu.roll(cur, 1, 0)
        nxt = pltpu.roll(cur, q - 1, 0)
        prev = jnp.concatenate([jnp.where(rowid == 0, prow, prev[:SUB]), prev[SUB:]], axis=0)
        nxt = jnp.concatenate([nxt[:q - SUB], jnp.where(rowid == SUB - 1, nrow, nxt[q - SUB:])], axis=0)
        y = prev * w0 + cur * w1 + nxt * w2 + bias
        if act:
            y = y + y * jnp.tanh(y)
        o_ref[0, pl.ds(r0, q), :] = y.astype(o_ref.dtype)
        return carry

    lax.fori_loop(0, n_chunks, body, 0)


def _shortconv_call(x, w, b, n_lat_chunks, act, out_dtype):
    B, T, C = x.shape
    cb = 256
    kern = functools.partial(_shortconv_kernel, n_lat_chunks=n_lat_chunks, n_chunks=T // SSD_CHUNK, act=act)
    return pl.pallas_call(
        kern,
        out_shape=jax.ShapeDtypeStruct((B, T, C), out_dtype),
        grid=(B, C // cb),
        in_specs=[pl.BlockSpec((1, T, cb), lambda b, j: (b, 0, j)),
                  pl.BlockSpec((3, cb), lambda b, j: (0, j)),
                  pl.BlockSpec((1, cb), lambda b, j: (0, j))],
        out_specs=pl.BlockSpec((1, T, cb), lambda b, j: (b, 0, j)),
        compiler_params=_cparams(("arbitrary", "arbitrary")),
        name="shortconv_act" if act else "shortconv",
    )(x, w, b.reshape(1, C))


def _ssd_kernel(x_ref, bt_ref, c_ref, z_ref, dtr_ref, alc_ref, dsk_ref, o_ref,
                yacc, ef_s, eb_s, sf_s, sb_s, df_s, db_s, hf_s, hb_s, *, n_lat_chunks, n_ctx_chunks):
    q = SSD_CHUNK
    r_heads = SSD_HPG
    gc = GROUP_COLS
    nch = n_lat_chunks + n_ctx_chunks
    li = lax.broadcasted_iota(jnp.int32, (q, q), 0)
    si = lax.broadcasted_iota(jnp.int32, (q, q), 1)
    low = si <= li
    upp = si >= li
    tri_u = jnp.where(upp, 1.0, 0.0).astype(BF16)
    a_col = -jnp.exp(alc_ref[0]) * LOG2E
    dsk = dsk_ref[0]
    is_fwd = lax.broadcasted_iota(jnp.int32, (2 * r_heads, 1), 0) < r_heads
    lane_head = lax.broadcasted_iota(jnp.int32, (1, gc), 1) // SSD_HEAD_DIM
    first_half = lax.broadcasted_iota(jnp.int32, (1, q), 1) < SSD_HEAD_DIM

    def by_head(parts):
        return jnp.concatenate([jnp.where(first_half, parts[0], parts[1]),
                                jnp.where(first_half, parts[2], parts[3])], axis=1)

    def phase_a(c, carry):
        r0 = pl.multiple_of(c * q, q)
        xh = x_ref[0, pl.ds(r0, q), :]
        cc = c_ref[0, pl.ds(r0, q), :]
        dtr = dtr_ref[0, 0, c]
        dta = dtr * a_col
        d1, d2, d3 = _split3(dta)
        pre = _dot(d1, tri_u) + (_dot(d2, tri_u) + _dot(d3, tri_u))
        tot = pre[:, q - 1:q]
        la = jnp.where(is_fwd, pre, tot - pre + dta)
        wrow = dtr * jnp.exp2(tot - la)
        dec = jnp.exp2(tot)
        bt = bt_ref[0, 0, c]
        cb = _dot(cc, bt)
        zero = jnp.zeros_like(xh)
        xm = [jnp.where(lane_head == r, xh, zero) for r in range(r_heads)]
        xst = jnp.concatenate(xm, axis=0)
        btf = bt.astype(F32)
        sts = []
        for d in range(2):
            bw = jnp.concatenate([(btf * wrow[d * r_heads + r:d * r_heads + r + 1, :]).astype(BF16)
                                  for r in range(r_heads)], axis=1)
            sts.append(_dot(bw, xst))
        st = jnp.concatenate(sts, axis=1)
        sf_s[c] = st[:, :gc]
        sb_s[c] = st[:, gc:]
        y = dsk * xh.astype(F32)
        for pair in range(r_heads // 2):
            efs, ebs = [], []
            for r in (2 * pair, 2 * pair + 1):
                colf = jnp.broadcast_to(la[r:r + 1, :], (q, q)).T
                colb = jnp.broadcast_to(la[r_heads + r:r_heads + r + 1, :], (q, q)).T
                wf = jnp.exp2(jnp.where(low, colf - la[r:r + 1, :], NEG_BIG)) * dtr[r:r + 1, :]
                wb = jnp.exp2(jnp.where(upp, colb - la[r_heads + r:r_heads + r + 1, :], NEG_BIG)) \
                    * dtr[r_heads + r:r_heads + r + 1, :]
                y = y + _dot(((wf + wb) * cb).astype(BF16), xm[r])
                efs.append(jnp.exp2(colf))
                ebs.append(jnp.exp2(colb))
            cols = slice(pair * q, (pair + 1) * q)
            ef_s[pl.ds(r0, q), cols] = jnp.where(first_half, efs[0], efs[1]).astype(ef_s.dtype)
            eb_s[pl.ds(r0, q), cols] = jnp.where(first_half, ebs[0], ebs[1]).astype(eb_s.dtype)
        yacc[pl.ds(r0, q), :] = y
        df_s[c] = jnp.broadcast_to(by_head([dec[r:r + 1, :] for r in range(r_heads)]), (SUB, gc))
        db_s[c] = jnp.broadcast_to(by_head([dec[r_heads + r:r_heads + r + 1, :] for r in range(r_heads)]),
                                   (SUB, gc))
        return carry

    lax.fori_loop(0, nch, phase_a, 0, unroll=2)

    hf_s[...] = jnp.zeros_like(hf_s)
    hb_s[...] = jnp.zeros_like(hb_s)

    def phase_b(j, carry):
        c_f = jnp.where(j < n_ctx_chunks, n_lat_chunks + j, j - n_ctx_chunks)
        c_b = nch - 1 - j
        for c, h_s, e_s, s_s, d_s in ((c_f, hf_s, ef_s, sf_s, df_s), (c_b, hb_s, eb_s, sb_s, db_s)):
            r0 = pl.multiple_of(c * q, q)
            h = h_s[...]
            ch = _dot(c_ref[0, pl.ds(r0, q), :], h.astype(BF16))
            yacc[pl.ds(r0, q), :] += ch * e_s[pl.ds(r0, q), :].astype(F32)
            h_s[...] = h * d_s[c][0:1, :] + s_s[c]
        return carry

    lax.fori_loop(0, nch, phase_b, 0, unroll=4)

    def phase_c(c, carry):
        r0 = pl.multiple_of(c * q, q)
        zz = z_ref[0, pl.ds(r0, q), :].astype(F32)
        o_ref[0, pl.ds(r0, q), :] = (yacc[pl.ds(r0, q), :] * _silu(zz)).astype(o_ref.dtype)
        return carry

    lax.fori_loop(0, nch, phase_c, 0, unroll=2)


def _ssd_call(xbc_c, bt, z, dt_row, alog_col, dsk, n_lat_chunks, n_ctx_chunks):
    B, T, _ = xbc_c.shape
    G = SSD_GROUPS
    gc = GROUP_COLS
    nch = T // SSD_CHUNK
    c_off = (SSD_WIDTH + SSD_GN) // SSD_STATE
    kern = functools.partial(_ssd_kernel, n_lat_chunks=n_lat_chunks, n_ctx_chunks=n_ctx_chunks)
    return pl.pallas_call(
        kern,
        out_shape=jax.ShapeDtypeStruct((B, T, SSD_WIDTH), BF16),
        grid=(B, G),
        in_specs=[pl.BlockSpec((1, T, gc), lambda b, g: (b, 0, g)),
                  pl.BlockSpec((1, 1, nch, SSD_STATE, SSD_CHUNK), lambda b, g: (b, g, 0, 0, 0)),
                  pl.BlockSpec((1, T, SSD_STATE), lambda b, g: (b, 0, c_off + g)),
                  pl.BlockSpec((1, T, gc), lambda b, g: (b, 0, g)),
                  pl.BlockSpec((1, 1, nch, 2 * SSD_HPG, SSD_CHUNK), lambda b, g: (b, g, 0, 0, 0)),
                  pl.BlockSpec((1, 2 * SSD_HPG, 1), lambda b, g: (g, 0, 0)),
                  pl.BlockSpec((1, 1, gc), lambda b, g: (g, 0, 0))],
        out_specs=pl.BlockSpec((1, T, gc), lambda b, g: (b, 0, g)),
        scratch_shapes=[pltpu.VMEM((T, gc), F32), pltpu.VMEM((T, gc), BF16), pltpu.VMEM((T, gc), BF16),
                        pltpu.VMEM((nch, SSD_STATE, gc), F32), pltpu.VMEM((nch, SSD_STATE, gc), F32),
                        pltpu.VMEM((nch, SUB, gc), F32), pltpu.VMEM((nch, SUB, gc), F32),
                        pltpu.VMEM((SSD_STATE, gc), F32), pltpu.VMEM((SSD_STATE, gc), F32)],
        compiler_params=_cparams(("arbitrary", "arbitrary"), VMEM_BIG),
        name="ssd",
    )(xbc_c, bt, xbc_c, z, dt_row, alog_col, dsk)


def _attn_kernel(lam_ref, q_ref, k_ref, v_ref, gd_ref, o_ref, vext, *, lam_init):
    hw = 2 * DIFF_DIM

    @pl.when(pl.program_id(2) == 0)
    def _():
        col = lax.broadcasted_iota(jnp.int32, (v_ref.shape[1], hw), 1)
        vext[:, :hw] = v_ref[0]
        vext[:, hw:] = jnp.where(col == 0, 1.0, 0.0).astype(BF16)

    lane = lax.broadcasted_iota(jnp.int32, (1, hw), 1)
    qt = q_ref[0]
    kt = k_ref[0]
    zero = jnp.zeros_like(qt)
    scores = [lax.dot_general(jnp.where((lane >= DIFF_DIM) == (m == 1), qt, zero), kt,
                              (((1,), (1,)), ((), ())), preferred_element_type=F32) for m in range(2)]
    heads = []
    for s in scores:
        p = jnp.exp2(s - jnp.max(s, axis=1, keepdims=True)).astype(BF16)
        ov = _dot(p, vext[...])
        heads.append(ov[:, :hw] / ov[:, hw:hw + 1])

    lv = lam_ref[...]
    lam = (jnp.exp(jnp.sum(lv[0:1] * lv[1:2], axis=1, keepdims=True))
           - jnp.exp(jnp.sum(lv[2:3] * lv[3:4], axis=1, keepdims=True)) + lam_init)
    o = heads[0] - lam * heads[1]
    o = o * lax.rsqrt(jnp.mean(o * o, axis=1, keepdims=True) + RMS_EPS) * (1.0 - lam_init)
    o_ref[0] = (o * _silu(gd_ref[0].astype(F32))).astype(o_ref.dtype)


def _attn_call(lam_vecs, q, k, v, gd, tq, q_tile0, n_q_tiles, kv_rows, kv_blk, lam_init):
    B = q.shape[0]
    hw = 2 * DIFF_DIM
    kern = functools.partial(_attn_kernel, lam_init=lam_init)
    return pl.pallas_call(
        kern,
        out_shape=jax.ShapeDtypeStruct((B, n_q_tiles * tq, DIFF_WIDTH), BF16),
        grid=(B, DIFF_HEADS, n_q_tiles),
        in_specs=[pl.BlockSpec((4, DIFF_DIM), lambda b, h, i: (0, 0)),
                  pl.BlockSpec((1, tq, hw), lambda b, h, i: (b, q_tile0 + i, h)),
                  pl.BlockSpec((1, kv_rows, hw), lambda b, h, i: (b, kv_blk, h)),
                  pl.BlockSpec((1, kv_rows, hw), lambda b, h, i: (b, kv_blk, h)),
                  pl.BlockSpec((1, tq, hw), lambda b, h, i: (b, q_tile0 + i, h))],
        out_specs=pl.BlockSpec((1, tq, hw), lambda b, h, i: (b, i, h)),
        scratch_shapes=[pltpu.VMEM((kv_rows, 2 * hw), BF16)],
        compiler_params=_cparams(("arbitrary", "arbitrary", "arbitrary"), VMEM_BIG),
        name="diff_attn",
    )(lam_vecs, q, k, v, gd)


def _hyfilt_kernel(feat_ref, t_ref, w1_ref, b1_ref, w2_ref, b2_ref, fr_ref, w3a_ref, b3a_ref,
                   w3b_ref, b3b_ref, dl_ref, oa_ref, ob_ref):
    fr = fr_ref[...]
    h1 = jnp.sin(fr * (_dot3(feat_ref[...], w1_ref[...]) + b1_ref[...]))
    h2 = jnp.sin(fr * (_dot3(h1, w2_ref[...]) + b2_ref[...]))
    win = jnp.exp(-t_ref[...] * dl_ref[...])
    ca = (_dot3(h2, w3a_ref[...]) + b3a_ref[...]) * win
    cb = (_dot3(h2, w3b_ref[...]) + b3b_ref[...]) * win
    rowid = lax.broadcasted_iota(jnp.int32, (ca.shape[0], 1), 0)
    cb = jnp.where(rowid == 0, 0.0, cb)
    energy = jnp.sum(ca * ca, axis=0, keepdims=True) + jnp.sum(cb * cb, axis=0, keepdims=True)
    sc = lax.rsqrt(energy + RMS_EPS)
    oa_ref[0] = ca * sc
    ob_ref[0] = cb * sc


def _hyfilt_call(feats, tpos, w1p, b1, w2, b2, freq, w3, b3, deltas):
    L = feats.shape[0]
    cb = HY_CB
    ncb = HY_WIDTH // cb
    hid = HY_HIDDEN

    def full(shape):
        return pl.BlockSpec(shape, lambda o, j: (0,) * len(shape))

    def w3spec(d):
        return pl.BlockSpec((hid, cb), lambda o, j: (0, (d * HY_ORDER + o) * ncb + j))

    def b3spec(d):
        return pl.BlockSpec((1, cb), lambda o, j: (0, (d * HY_ORDER + o) * ncb + j))

    out = jax.ShapeDtypeStruct((HY_ORDER, L, HY_WIDTH), F32)
    ospec = pl.BlockSpec((1, L, cb), lambda o, j: (o, 0, j))
    return pl.pallas_call(
        _hyfilt_kernel,
        out_shape=(out, out),
        grid=(HY_ORDER, ncb),
        in_specs=[full((L, hid)), full((L, 1)), full((hid, hid)), full((1, hid)), full((hid, hid)),
                  full((1, hid)), full((1, hid)), w3spec(0), b3spec(0), w3spec(1), b3spec(1),
                  pl.BlockSpec((1, cb), lambda o, j: (0, j))],
        out_specs=(ospec, ospec),
        compiler_params=_cparams(("arbitrary", "arbitrary"), VMEM_BIG),
        name="hy_filter",
    )(feats, tpos, w1p, b1, w2, b2, freq, w3, b3, w3, b3, deltas)


def _twiddle(twc_ref, tws_ref, k1, width):
    reps = width // LANES
    c = jnp.concatenate([twc_ref[k1]] * reps, axis=1)
    s = jnp.concatenate([tws_ref[k1]] * reps, axis=1)
    return c, s


def _hyspec_kernel(fa_ref, fb_ref, gl1_ref, twc_ref, tws_ref, f64_ref, kr_ref, ki_ref, wsr, wsi, *, k1c):
    kc = pl.program_id(2)
    cw = fa_ref.shape[-1]
    gl1 = gl1_ref[0]
    for j in range(DFT2 // SUB):
        xa = fa_ref[0, :, SUB * j:SUB * (j + 1), :]
        xb = fb_ref[0, :, SUB * j:SUB * (j + 1), :]
        h = xa.shape[0]
        xs = jnp.concatenate([xa.reshape(h * SUB, cw), xb.reshape(h * SUB, cw)], axis=1).astype(BF16)
        a = _dot(gl1, xs)
        wsr[:, SUB * j:SUB * (j + 1), :] = a[:k1c * SUB].reshape(k1c, SUB, 2 * cw)
        wsi[:, SUB * j:SUB * (j + 1), :] = a[k1c * SUB:].reshape(k1c, SUB, 2 * cw)
    f64 = f64_ref[...]

    def body(k, carry):
        c, s = _twiddle(twc_ref, tws_ref, kc * k1c + k, 2 * cw)
        ar = wsr[k]
        ai = wsi[k]
        br = ar * c + ai * s
        bi = ai * c - ar * s
        x = _dot(f64, jnp.concatenate([br, bi], axis=0).astype(BF16))
        xr = x[:DFT2]
        xi = x[DFT2:]
        r0 = pl.multiple_of(k * DFT2, DFT2)
        kr_ref[0, pl.ds(r0, DFT2), :] = (xr[:, :cw] + xr[:, cw:]).astype(kr_ref.dtype)
        ki_ref[0, pl.ds(r0, DFT2), :] = (xi[:, :cw] - xi[:, cw:]).astype(ki_ref.dtype)
        return carry

    lax.fori_loop(0, k1c, body, 0, unroll=math.gcd(HY_UNROLL, k1c))


def _hyspec_call(fa, fb, tabs):
    _, L, _ = fa.shape
    H = L // DFT2
    k1c, k1n = tabs["k1c"], tabs["k1n"]
    nkc = k1n // k1c
    cb = HY_CB
    ncb = HY_WIDTH // cb
    fa4 = fa.reshape(HY_ORDER, H, DFT2, HY_WIDTH)
    fb4 = fb.reshape(HY_ORDER, H, DFT2, HY_WIDTH)
    out = jax.ShapeDtypeStruct((HY_ORDER, k1n * DFT2, HY_WIDTH), BF16)
    ospec = pl.BlockSpec((1, k1c * DFT2, cb), lambda o, j, kc: (o, kc, j))
    fspec = pl.BlockSpec((1, H, DFT2, cb), lambda o, j, kc: (o, 0, 0, j))
    kern = functools.partial(_hyspec_kernel, k1c=k1c)
    return pl.pallas_call(
        kern,
        out_shape=(out, out),
        grid=(HY_ORDER, ncb, nkc),
        in_specs=[fspec, fspec,
                  pl.BlockSpec((1, 2 * k1c * SUB, H * SUB), lambda o, j, kc: (kc, 0, 0)),
                  pl.BlockSpec((k1n, DFT2, LANES), lambda o, j, kc: (0, 0, 0), pipeline_mode=pl.Buffered(1)),
                  pl.BlockSpec((k1n, DFT2, LANES), lambda o, j, kc: (0, 0, 0), pipeline_mode=pl.Buffered(1)),
                  pl.BlockSpec((2 * DFT2, 2 * DFT2), lambda o, j, kc: (0, 0))],
        out_specs=(ospec, ospec),
        scratch_shapes=[pltpu.VMEM((k1c, DFT2, 2 * cb), F32), pltpu.VMEM((k1c, DFT2, 2 * cb), F32)],
        compiler_params=_cparams(("arbitrary", "arbitrary", "arbitrary"), VMEM_BIG),
        name="hy_spectrum",
    )(fa4, fb4, tabs["gl1"], tabs["twc"], tabs["tws"], tabs["f64"])


def _hyconv_kernel(*refs, k1c, nkc, n_fft, final):
    u_ref, g_ref = refs[0], refs[1]
    pos = 2
    gh_ref = None
    if final:
        gh_ref = refs[pos]
        pos += 1
    kr_ref, ki_ref, bias_ref, gl1_ref, gl1i_ref, twc_ref, tws_ref, f64_ref, f64i_ref = refs[pos:pos + 9]
    o_ref, wsr, wsi, acc = refs[pos + 9:pos + 13]
    kc = pl.program_id(2)
    cw = u_ref.shape[-1]
    h = u_ref.shape[1]

    gl1 = gl1_ref[0]
    for j in range(DFT2 // SUB):
        xs = u_ref[0, :, SUB * j:SUB * (j + 1), :].reshape(h * SUB, cw).astype(BF16)
        a = _dot(gl1, xs)
        wsr[:, SUB * j:SUB * (j + 1), :] = a[:k1c * SUB].reshape(k1c, SUB, cw)
        wsi[:, SUB * j:SUB * (j + 1), :] = a[k1c * SUB:].reshape(k1c, SUB, cw)
    f64 = f64_ref[...]
    f64i = f64i_ref[...]

    def body(k, carry):
        c, s = _twiddle(twc_ref, tws_ref, kc * k1c + k, cw)
        ar = wsr[k]
        ai = wsi[k]
        br = ar * c + ai * s
        bi = ai * c - ar * s
        x = _dot(f64, jnp.concatenate([br, bi], axis=0).astype(BF16))
        xr = x[:DFT2]
        xi = x[DFT2:]
        r0 = pl.multiple_of(k * DFT2, DFT2)
        kr = kr_ref[0, pl.ds(r0, DFT2), :].astype(F32)
        ki = ki_ref[0, pl.ds(r0, DFT2), :].astype(F32)
        yr = xr * kr - xi * ki
        yi = xr * ki + xi * kr
        cc = _dot(f64i, jnp.concatenate([yr, yi], axis=0).astype(BF16))
        cr = cc[:DFT2]
        ci = cc[DFT2:]
        wsr[k] = cr * c - ci * s
        wsi[k] = ci * c + cr * s
        return carry

    lax.fori_loop(0, k1c, body, 0, unroll=math.gcd(HY_UNROLL, k1c))

    @pl.when(kc == 0)
    def _():
        acc[...] = jnp.zeros_like(acc)

    gl1i = gl1i_ref[0]
    for j in range(DFT2 // SUB):
        dr = wsr[:, SUB * j:SUB * (j + 1), :].reshape(k1c * SUB, cw)
        di = wsi[:, SUB * j:SUB * (j + 1), :].reshape(k1c * SUB, cw)
        yj = _dot(gl1i, jnp.concatenate([dr, di], axis=0).astype(BF16))
        acc[:, SUB * j:SUB * (j + 1), :] += yj.reshape(h, SUB, cw)

    @pl.when(kc == nkc - 1)
    def _():
        u = u_ref[0]
        y = g_ref[0] * (acc[...] * (1.0 / n_fft) + bias_ref[0] * u)
        if final:
            y = y * _silu(gh_ref[0].astype(F32))
        o_ref[0] = y.astype(o_ref.dtype)


def _hyconv_call(u4, u_rb, u_cb, g4, g_rb, g_cb, gh4, gh_rb, kr, ki, bias, order, tabs, H, out_dtype):
    B = u4.shape[0]
    k1c, k1n = tabs["k1c"], tabs["k1n"]
    nkc = k1n // k1c
    cb = HY_CB
    final = gh4 is not None

    def seq_spec(rb, off):
        return pl.BlockSpec((1, H, DFT2, cb), lambda b, j, kc: (b, rb, 0, off + j))

    in_specs = [seq_spec(u_rb, u_cb), seq_spec(g_rb, g_cb)]
    args = [u4, g4]
    if final:
        in_specs.append(seq_spec(gh_rb, 0))
        args.append(gh4)
    in_specs += [
        pl.BlockSpec((1, k1c * DFT2, cb), lambda b, j, kc: (order, kc, j)),
        pl.BlockSpec((1, k1c * DFT2, cb), lambda b, j, kc: (order, kc, j)),
        pl.BlockSpec((1, 1, cb), lambda b, j, kc: (order, 0, j)),
        pl.BlockSpec((1, 2 * k1c * SUB, H * SUB), lambda b, j, kc: (kc, 0, 0)),
        pl.BlockSpec((1, H * SUB, 2 * k1c * SUB), lambda b, j, kc: (kc, 0, 0)),
        pl.BlockSpec((k1n, DFT2, LANES), lambda b, j, kc: (0, 0, 0), pipeline_mode=pl.Buffered(1)),
        pl.BlockSpec((k1n, DFT2, LANES), lambda b, j, kc: (0, 0, 0), pipeline_mode=pl.Buffered(1)),
        pl.BlockSpec((2 * DFT2, 2 * DFT2), lambda b, j, kc: (0, 0)),
        pl.BlockSpec((2 * DFT2, 2 * DFT2), lambda b, j, kc: (0, 0)),
    ]
    args += [kr, ki, bias.reshape(HY_ORDER, 1, HY_WIDTH), tabs["gl1"], tabs["gl1i"], tabs["twc"], tabs["tws"],
             tabs["f64"], tabs["f64i"]]
    kern = functools.partial(_hyconv_kernel, k1c=k1c, nkc=nkc, n_fft=float(2 * H * DFT2), final=final)
    return pl.pallas_call(
        kern,
        out_shape=jax.ShapeDtypeStruct((B, H, DFT2, HY_WIDTH), out_dtype),
        grid=(B, HY_WIDTH // cb, nkc),
        in_specs=in_specs,
        out_specs=pl.BlockSpec((1, H, DFT2, cb), lambda b, j, kc: (b, 0, 0, j)),
        scratch_shapes=[pltpu.VMEM((k1c, DFT2, cb), F32), pltpu.VMEM((k1c, DFT2, cb), F32),
                        pltpu.VMEM((H, DFT2, cb), F32)],
        compiler_params=_cparams(("arbitrary", "arbitrary", "arbitrary"), VMEM_BIG),
        name="hy_conv",
    )(*args)


def _dft_tables(H):
    n1 = 2 * H
    n = n1 * DFT2
    if n1 >= 2 * HY_K1C:
        k1c = HY_K1C
    else:
        k1c = n1 // 2 + 1
    nkc = -(-(n1 // 2 + 1) // k1c)
    k1n = nkc * k1c
    k1 = np.arange(k1n, dtype=np.float64)
    live = (k1 <= n1 // 2).astype(np.float64)
    wgt = np.where((k1 == 0) | (k1 == n1 // 2), 1.0, 2.0) * live
    th = 2.0 * np.pi * np.outer(k1, np.arange(H, dtype=np.float64)) / n1
    eye = np.eye(SUB)
    gre = np.kron(np.cos(th) * live[:, None], eye).reshape(nkc, k1c * SUB, H * SUB)
    gim = np.kron(-np.sin(th) * live[:, None], eye).reshape(nkc, k1c * SUB, H * SUB)
    gl1 = np.concatenate([gre, gim], axis=1)
    gire = np.kron((np.cos(th) * wgt[:, None]).T, eye).reshape(H * SUB, nkc, k1c * SUB).transpose(1, 0, 2)
    giim = np.kron((-np.sin(th) * wgt[:, None]).T, eye).reshape(H * SUB, nkc, k1c * SUB).transpose(1, 0, 2)
    gl1i = np.concatenate([gire, giim], axis=2)
    ph = 2.0 * np.pi * np.outer(k1, np.arange(DFT2, dtype=np.float64)) / n
    twc = np.repeat(np.cos(ph)[:, :, None], LANES, axis=2)
    tws = np.repeat(np.sin(ph)[:, :, None], LANES, axis=2)
    ps = 2.0 * np.pi * np.outer(np.arange(DFT2), np.arange(DFT2)) / DFT2
    c64, s64 = np.cos(ps), np.sin(ps)
    f64 = np.block([[c64, s64], [-s64, c64]])
    f64i = np.block([[c64, -s64], [s64, c64]])
    as_bf = lambda a: jnp.asarray(a, F32).astype(BF16)
    return dict(k1c=k1c, k1n=k1n, gl1=as_bf(gl1), gl1i=as_bf(gl1i), twc=jnp.asarray(twc, F32),
                tws=jnp.asarray(tws, F32), f64=as_bf(f64), f64i=as_bf(f64i))


def _hy_features(L):
    t = np.linspace(0.0, 1.0, L)[:, None]
    wpos = (2.0 * math.pi / L) * np.arange(L, dtype=np.float64)[:, None]
    bands = np.linspace(1e-4, HY_BANDS - 1, HY_BANDS)
    feats = np.concatenate([t, np.cos(wpos * bands), -np.sin(wpos * bands)], -1)
    feats = np.pad(feats, ((0, 0), (0, HY_HIDDEN - HY_EMB)))
    return jnp.asarray(feats, F32), jnp.asarray(t, F32)


def _outproj_kernel(y_ref, od_ref, oh_ref, h_ref, mod_ref, w_ref, nw_ref, g_ref, b_ref, o_ref, *, alpha):
    nb, tm, d = h_ref.shape
    sw = y_ref.shape[-1]
    dw = od_ref.shape[-1]
    ys, ods, ohs = [], [], []
    for j in range(nb):
        y = y_ref[j].astype(F32)
        y = y * lax.rsqrt(jnp.mean(y * y, axis=-1, keepdims=True) + RMS_EPS) * nw_ref[...]
        ys.append(y.astype(BF16))
        ods.append(od_ref[j])
        ohs.append(oh_ref[j])
    cat = (lambda xs: jnp.concatenate(xs, axis=0)) if nb > 1 else (lambda xs: xs[0])
    o = (_dot(cat(ys), w_ref[0:sw, :]) + _dot(cat(ods), w_ref[sw:sw + dw, :])
         + _dot(cat(ohs), w_ref[sw + dw:, :]))
    for j in range(nb):
        gate = mod_ref[j, 0, :, 2 * d:3 * d]
        r = alpha * h_ref[j] + gate * o[j * tm:(j + 1) * tm]
        rc = r - jnp.mean(r, axis=-1, keepdims=True)
        var = jnp.mean(rc * rc, axis=-1, keepdims=True)
        o_ref[j] = rc * lax.rsqrt(var + LN_EPS) * g_ref[...] + b_ref[...]


def _outproj_call(yg, od, oh, hcat, modcat, w_out, norm_w, ln_g, ln_b, n_lat_tiles, n_tiles, alpha):
    B, T, D = hcat.shape
    nb = 2 if B % 2 == 0 else 1

    def kind(i):
        return jnp.where(i >= n_lat_tiles, 1, 0)

    def tok(w):
        return pl.BlockSpec((nb, TM, w), lambda b, i: (b, i, 0))

    def vec(w):
        return pl.BlockSpec((1, w), lambda b, i: (0, 0))

    kern = functools.partial(_outproj_kernel, alpha=alpha)
    return pl.pallas_call(
        kern,
        out_shape=jax.ShapeDtypeStruct((B, n_tiles * TM, D), F32),
        grid=(B // nb, n_tiles),
        in_specs=[tok(SSD_WIDTH), tok(DIFF_WIDTH), tok(HY_WIDTH), tok(D),
                  pl.BlockSpec((nb, 1, 1, 3 * D), lambda b, i: (b, kind(i), 0, 0)),
                  pl.BlockSpec(w_out.shape, lambda b, i: (0, 0)),
                  vec(SSD_WIDTH), vec(D), vec(D)],
        out_specs=tok(D),
        compiler_params=_cparams(("arbitrary", "arbitrary"), VMEM_BIG),
        name="outproj",
    )(yg, od, oh, hcat, modcat, w_out, norm_w.reshape(1, -1), ln_g.reshape(1, -1), ln_b.reshape(1, -1))


def _rope_tables(L, Lc):
    rows = np.repeat(np.arange(L // GRID_W), GRID_W).astype(np.float64)
    cols = np.tile(np.arange(GRID_W), L // GRID_W).astype(np.float64)
    inv = ROPE_BASE ** (-np.arange(ROPE_FREQS, dtype=np.float64) / ROPE_FREQS)
    lane = np.arange(LANES)
    f = lane % ROPE_FREQS
    is_row = (lane % DIFF_DIM) < (DIFF_DIM // 2)
    ang = np.where(is_row[None, :], rows[:, None] * inv[f][None, :], cols[:, None] * inv[f][None, :])
    cos = np.concatenate([np.cos(ang), np.ones((Lc, LANES))], 0)
    sin = np.concatenate([np.sin(ang), np.zeros((Lc, LANES))], 0)
    return jnp.asarray(cos, F32), jnp.asarray(sin, F32)


def _dt_perm():
    g, d, r = np.meshgrid(np.arange(SSD_GROUPS), np.arange(2), np.arange(SSD_HPG), indexing="ij")
    return (d * SSD_HEADS + g * SSD_HPG + r).reshape(-1)


def _layout_w_in(w):
    o = IN_OFFS
    z, xbc, dt = w[:, o[0]:o[1]], w[:, o[1]:o[2]], w[:, o[2]:o[3]]
    q, k, v, gd = w[:, o[3]:o[4]] * (DIFF_SCALE * LOG2E), w[:, o[4]:o[5]], w[:, o[5]:o[6]], w[:, o[6]:o[7]]
    hp, gh = w[:, o[7]:o[8]], w[:, o[8]:o[9]]
    dtp = jnp.pad(dt[:, _dt_perm()], ((0, 0), (0, LANES - SSD_DT)))
    cat = jnp.concatenate([z, xbc, q, k, v, gd, hp, gh, dtp], axis=1)
    return cat.astype(BF16)


def kernel(x, c, ctx, c_ctx, w_ada, b_ada, w_in, ssd_conv_w, ssd_conv_b, ssd_dt_bias, ssd_a_log, ssd_d,
           ssd_norm_w, diff_lambda, hy_conv_w, hy_conv_b, hy_w1, hy_b1, hy_w2, hy_b2, hy_w3, hy_b3,
           hy_freq, hy_bias, w_out, ln_g, ln_b):
    B, L, D = x.shape
    Lc = ctx.shape[1]
    depth = w_in.shape[0]
    T = L + Lc
    assert L % TM == 0 and Lc % TM == 0 and L % GRID_W == 0 and D == D_MODEL
    n_lat_tiles, n_tiles = L // TM, T // TM
    n_lat_chunks, n_ctx_chunks = L // SSD_CHUNK, Lc // SSD_CHUNK
    nch = n_lat_chunks + n_ctx_chunks
    G, R8 = SSD_GROUPS, 2 * SSD_HPG
    alpha = (2 * depth) ** 0.25
    H_lat, H_ctx = L // DFT2, Lc // DFT2
    assert H_lat % H_ctx == 0 and L % Lc == 0

    rope_cos, rope_sin = _rope_tables(L, Lc)
    tabs_lat, tabs_ctx = _dft_tables(H_lat), _dft_tables(H_ctx)
    feats_lat, t_lat = _hy_features(L)
    feats_ctx, t_ctx = _hy_features(Lc)
    deltas = jnp.asarray(np.abs(np.linspace(HY_MIN_DECAY, HY_MAX_DECAY, HY_WIDTH))[None, :], F32)

    rows = -(-(B + 1) // SUB) * SUB
    cc = jnp.zeros((rows, D), F32).at[:B].set(c).at[B].set(c_ctx)
    mod_all = _ada_call(cc, w_ada, b_ada)

    hcat = jnp.concatenate([x, ctx], axis=1)
    perm = _dt_perm()
    for l in range(depth):
        need_ctx = l < depth - 1
        mod = mod_all[l]
        modcat = jnp.stack([mod[:B], jnp.broadcast_to(mod[B], (B, 3 * D))], axis=1)[:, :, None, :]
        dtb = jnp.pad(ssd_dt_bias[l].reshape(-1)[perm], (0, LANES - SSD_DT)).reshape(1, LANES)
        z, xbc, q, k, v, gd, hp, gh, dt = _inproj_call(
            hcat, modcat, _layout_w_in(w_in[l]), dtb, rope_cos, rope_sin, n_lat_tiles)

        xbc_c = _shortconv_call(xbc, ssd_conv_w[l], ssd_conv_b[l], n_lat_chunks, True, BF16)
        dt_row = dt[:, :, :SSD_DT].reshape(B, nch, SSD_CHUNK, G, R8).transpose(0, 3, 1, 4, 2)
        alog = ssd_a_log[l].reshape(2, G, SSD_HPG).transpose(1, 0, 2).reshape(G, R8)
        dsk = jnp.repeat(ssd_d[l].reshape(G, SSD_HPG), SSD_HEAD_DIM, axis=1).reshape(G, 1, GROUP_COLS)
        bt = xbc_c[:, :, SSD_WIDTH:SSD_WIDTH + SSD_GN].reshape(B, nch, SSD_CHUNK, G, SSD_STATE)
        bt = bt.transpose(0, 3, 1, 4, 2)
        yg = _ssd_call(xbc_c, bt, z, dt_row, alog.reshape(G, R8, 1), dsk, n_lat_chunks, n_ctx_chunks)

        lam_init = 0.8 - 0.6 * math.exp(-0.3 * l)
        tq = math.gcd(TQ, L)
        od = _attn_call(diff_lambda[l], q, k, v, gd, tq, 0, L // tq, T, 0, lam_init)
        if need_ctx:
            tqc = math.gcd(tq, Lc)
            od_c = _attn_call(diff_lambda[l], q, k, v, gd, tqc, L // tqc, Lc // tqc, Lc, L // Lc, lam_init)
            od = jnp.concatenate([od, od_c], axis=1)

        hpc = _shortconv_call(hp, hy_conv_w[l], hy_conv_b[l], n_lat_chunks, False, F32)
        hp4 = hpc.reshape(B, T // DFT2, DFT2, 3 * HY_WIDTH)
        gh4 = gh.reshape(B, T // DFT2, DFT2, HY_WIDTH)
        w1p = jnp.pad(hy_w1[l], ((0, HY_HIDDEN - HY_EMB), (0, 0)))
        filt_args = (w1p, hy_b1[l].reshape(1, -1), hy_w2[l], hy_b2[l].reshape(1, -1), hy_freq[l].reshape(1, -1),
                     hy_w3[l], hy_b3[l].reshape(1, -1), deltas)
        ncb = HY_WIDTH // HY_CB
        seqs = [(feats_lat, t_lat, tabs_lat, 0, H_lat)]
        if need_ctx:
            seqs.append((feats_ctx, t_ctx, tabs_ctx, H_lat // H_ctx, H_ctx))
        ohs = []
        for feats, tpos, tabs, rb, H in seqs:
            fa, fb = _hyfilt_call(feats, tpos, *filt_args)
            kr, ki = _hyspec_call(fa, fb, tabs)
            zz = _hyconv_call(hp4, rb, 0, hp4, rb, ncb, None, 0, kr, ki, hy_bias[l], 0, tabs, H, F32)
            oh = _hyconv_call(zz, 0, 0, hp4, rb, 2 * ncb, gh4, rb, kr, ki, hy_bias[l], 1, tabs, H, BF16)
            ohs.append(oh.reshape(B, H * DFT2, HY_WIDTH))
        oh = jnp.concatenate(ohs, axis=1) if need_ctx else ohs[0]

        hcat = _outproj_call(yg, od, oh, hcat, modcat, w_out[l].astype(BF16), ssd_norm_w[l], ln_g[l], ln_b[l],
                             n_lat_tiles, n_tiles if need_ctx else n_lat_tiles, alpha)
    return hcat[:, :L]
```
